```python
import math
import jax, jax.numpy as jnp
from jax import lax
import numpy as np

D_MODEL = 1024
BATCH = 32
SEQ = 2048
DEPTH = 1
DEC_BATCH = 32
DEC_SEQ = 64
PAST_LEN = 1024

CHUNK = 64
MIX_WIDTH = D_MODEL
SB_HEAD_DIM = 64
SB_HEADS = (MIX_WIDTH // 2) // SB_HEAD_DIM
SB_WIDTH = SB_HEADS * SB_HEAD_DIM
SB_BLOCK = 128
SSM_HEAD_DIM = 64
SSM_WIDTH = MIX_WIDTH - SB_WIDTH
SSM_HEADS = SSM_WIDTH // SSM_HEAD_DIM
SSM_GROUPS = 2
SSM_STATE = 64
CONV_WIDTH = 4
CONV_CH = SSM_WIDTH + 2 * SSM_GROUPS * SSM_STATE
D_FF = 4 * D_MODEL
EPS = 1e-5
IN_PROJ = 3 * SB_WIDTH + SSM_WIDTH + CONV_CH + SSM_HEADS
SPLITS = (SB_WIDTH, 2 * SB_WIDTH, 3 * SB_WIDTH, 3 * SB_WIDTH + SSM_WIDTH,
          3 * SB_WIDTH + SSM_WIDTH + CONV_CH)

kernel_name = "stickbreak_ssd_hybrid_stream_step"


def rms_norm(x, w):
    xf = x.astype(jnp.float32)
    xf = xf * lax.rsqrt(jnp.mean(xf * xf, axis=-1, keepdims=True) + EPS)
    return (xf * w.astype(jnp.float32)).astype(x.dtype)


def stick_breaking_block(q, k, v, q_pos):
    k_pos = jnp.arange(k.shape[1])
    z = jnp.einsum("bqhd,bkhd->bhqk", q, k).astype(jnp.float32) * (SB_HEAD_DIM ** -0.5)
    visible = k_pos[None, :] < q_pos[:, None]
    log_beta = jax.nn.log_sigmoid(z)
    log_keep = jnp.where(visible, jax.nn.log_sigmoid(-z), 0.0)
    later = lax.cumsum(log_keep, axis=3, reverse=True) - log_keep
    attn = jnp.where(visible, jnp.exp(log_beta + later), 0.0)
    return jnp.einsum("bhqk,bkhd->bqhd", attn.astype(v.dtype), v)


def causal_conv(xbc, buf, w, b):
    L = xbc.shape[1]
    xp = jnp.concatenate([buf.astype(xbc.dtype), xbc], axis=1)
    out = b
    for i in range(CONV_WIDTH):
        out = out + xp[:, i:i + L] * w[i]
    return jax.nn.silu(out), xp[:, xp.shape[1] - (CONV_WIDTH - 1):]


def segsum(a):
    T = a.shape[-1]
    rep = jnp.broadcast_to(a[..., :, None], a.shape + (T,))
    rep = jnp.where(jnp.tril(jnp.ones((T, T), bool), -1), rep, 0.0)
    cs = jnp.cumsum(rep, axis=-2)
    return jnp.where(jnp.tril(jnp.ones((T, T), bool)), cs, -jnp.inf)


def ssd_scan(x, dt, A, Bm, Cm, h0):
    Bsz, L = x.shape[:2]
    Q = min(CHUNK, L)
    nc = L // Q
    rep = SSM_HEADS // SSM_GROUPS
    Bc = jnp.repeat(Bm, rep, axis=2).reshape(Bsz, nc, Q, SSM_HEADS, SSM_STATE)
    Cc = jnp.repeat(Cm, rep, axis=2).reshape(Bsz, nc, Q, SSM_HEADS, SSM_STATE)
    xc = (x * dt[..., None]).reshape(Bsz, nc, Q, SSM_HEADS, SSM_HEAD_DIM)
    a = (dt * A).reshape(Bsz, nc, Q, SSM_HEADS).transpose(0, 3, 1, 2)
    a_cum = jnp.cumsum(a, axis=-1)
    decay_in = jnp.exp(segsum(a))
    y_diag = jnp.einsum("bclhn,bcshn,bhcls,bcshp->bclhp", Cc, Bc, decay_in, xc)
    decay_to_end = jnp.exp(a_cum[..., -1:] - a_cum)
    chunk_states = jnp.einsum("bclhn,bhcl,bclhp->bchpn", Bc, decay_to_end, xc)
    chunk_decay = jnp.exp(a_cum[..., -1])

    def step(h, inp):
        st, dec = inp
        return dec[..., None, None] * h + st, h

    h_final, h_prev = lax.scan(step, h0, (chunk_states.transpose(1, 0, 2, 3, 4),
                                          chunk_decay.transpose(2, 0, 1)))
    h_prev = h_prev.transpose(1, 0, 2, 3, 4)
    y_off = jnp.einsum("bclhn,bchpn,bhcl->bclhp", Cc, h_prev, jnp.exp(a_cum))
    return (y_diag + y_off).reshape(Bsz, L, SSM_HEADS, SSM_HEAD_DIM), h_final


def mamba2_mixer(z, xbc, dt_raw, conv_buf, h0, conv_w, conv_b, dt_bias, a_log, d_skip, ssm_norm_w):
    Bsz, L = z.shape[:2]
    xbc, new_buf = causal_conv(xbc, conv_buf, conv_w, conv_b)
    xf = xbc.astype(jnp.float32)
    xs = xf[..., :SSM_WIDTH].reshape(Bsz, L, SSM_HEADS, SSM_HEAD_DIM)
    Bm = xf[..., SSM_WIDTH:SSM_WIDTH + SSM_GROUPS * SSM_STATE].reshape(Bsz, L, SSM_GROUPS, SSM_STATE)
    Cm = xf[..., SSM_WIDTH + SSM_GROUPS * SSM_STATE:].reshape(Bsz, L, SSM_GROUPS, SSM_STATE)
    dt = jax.nn.softplus(dt_raw.astype(jnp.float32) + dt_bias.astype(jnp.float32))
    A = -jnp.exp(a_log.astype(jnp.float32))
    y, h_final = ssd_scan(xs, dt, A, Bm, Cm, h0.astype(jnp.float32))
    y = y + d_skip.astype(jnp.float32)[:, None] * xs
    y = y.reshape(Bsz, L, SSM_WIDTH) * jax.nn.silu(z.astype(jnp.float32))
    yg = y.reshape(Bsz, L, SSM_GROUPS, SSM_WIDTH // SSM_GROUPS)
    yg = yg * lax.rsqrt(jnp.mean(yg * yg, axis=-1, keepdims=True) + EPS)
    y = yg.reshape(Bsz, L, SSM_WIDTH) * ssm_norm_w.astype(jnp.float32)
    return y.astype(z.dtype), new_buf, h_final


def layer_forward(x, k_past, v_past, conv_buf, h0, norm1_w, w_in, conv_w, conv_b,
                  dt_bias, a_log, d_skip, ssm_norm_w, w_out, norm2_w, w_up, w_down):
    Bsz, L, _ = x.shape
    xn = rms_norm(x, norm1_w)
    proj = jnp.einsum("bld,de->ble", xn, w_in)
    q, k, v, z, xbc, dt_raw = jnp.split(proj, SPLITS, axis=-1)
    q = q.reshape(Bsz, L, SB_HEADS, SB_HEAD_DIM)
    k = k.reshape(Bsz, L, SB_HEADS, SB_HEAD_DIM)
    v = v.reshape(Bsz, L, SB_HEADS, SB_HEAD_DIM)
    if k_past is None:
        k_all, v_all, q0 = k, v, 0
    else:
        k_all = jnp.concatenate([k_past.astype(k.dtype), k], axis=1)
        v_all = jnp.concatenate([v_past.astype(v.dtype), v], axis=1)
        q0 = k_past.shape[1]
    q_pos = q0 + jnp.arange(L)
    if L > SB_BLOCK and L % SB_BLOCK == 0:
        nblk = L // SB_BLOCK
        qb = q.reshape(Bsz, nblk, SB_BLOCK, SB_HEADS, SB_HEAD_DIM).swapaxes(0, 1)
        pb = q_pos.reshape(nblk, SB_BLOCK)
        o = lax.map(lambda qp: stick_breaking_block(qp[0], k_all, v_all, qp[1]), (qb, pb))
        o = o.swapaxes(0, 1).reshape(Bsz, L, SB_WIDTH)
    else:
        o = stick_breaking_block(q, k_all, v_all, q_pos).reshape(Bsz, L, SB_WIDTH)
    y_ssm, new_buf, h_final = mamba2_mixer(z, xbc, dt_raw, conv_buf, h0, conv_w, conv_b,
                                           dt_bias, a_log, d_skip, ssm_norm_w)
    mix = jnp.concatenate([o.astype(x.dtype), y_ssm], axis=-1)
    x = x + jnp.einsum("ble,ed->bld", mix, w_out)
    hn = rms_norm(x, norm2_w)
    u = jnp.square(jax.nn.relu(jnp.einsum("bld,df->blf", hn, w_up)))
    x = x + jnp.einsum("blf,fd->bld", u, w_down)
    return x, k, v, new_buf, h_final


def setup_inputs(seed: int = 0) -> dict:
    key = jax.random.key(seed)
    ks = jax.random.split(key, 24)
    f32 = jnp.float32

    def nrm(k, shape, scale=1.0):
        return jax.random.normal(k, shape, f32) * scale

    dt0 = jnp.exp(jax.random.uniform(ks[10], (DEPTH, SSM_HEADS), f32,
                                     math.log(1e-3), math.log(1e-1)))
    return {
        "x_prompt": nrm(ks[0], (BATCH, SEQ, D_MODEL)),
        "x_sample": nrm(ks[1], (DEC_BATCH, DEC_SEQ, D_MODEL)),
        "cache_k": nrm(ks[2], (DEPTH, DEC_BATCH, PAST_LEN, SB_HEADS, SB_HEAD_DIM)),
        "cache_v": nrm(ks[3], (DEPTH, DEC_BATCH, PAST_LEN, SB_HEADS, SB_HEAD_DIM)),
        "state_conv": nrm(ks[4], (DEPTH, DEC_BATCH, CONV_WIDTH - 1, CONV_CH)),
        "state_ssm": nrm(ks[5], (DEPTH, DEC_BATCH, SSM_HEADS, SSM_HEAD_DIM, SSM_STATE), 0.5),
        "norm1_w": 1.0 + nrm(ks[6], (DEPTH, D_MODEL), 0.02),
        "w_in": nrm(ks[7], (DEPTH, D_MODEL, IN_PROJ), D_MODEL ** -0.5),
        "conv_w": nrm(ks[8], (DEPTH, CONV_WIDTH, CONV_CH), CONV_WIDTH ** -0.5),
        "conv_b": nrm(ks[9], (DEPTH, CONV_CH), 0.01),
        "dt_bias": dt0 + jnp.log(-jnp.expm1(-dt0)),
        "a_log": jnp.log(jax.random.uniform(ks[11], (DEPTH, SSM_HEADS), f32, 1.0, 16.0)),
        "d_skip": 1.0 + nrm(ks[12], (DEPTH, SSM_HEADS), 0.02),
        "ssm_norm_w": 1.0 + nrm(ks[13], (DEPTH, SSM_WIDTH), 0.02),
        "w_out": nrm(ks[14], (DEPTH, MIX_WIDTH, D_MODEL), MIX_WIDTH ** -0.5),
        "norm2_w": 1.0 + nrm(ks[15], (DEPTH, D_MODEL), 0.02),
        "w_up": nrm(ks[16], (DEPTH, D_MODEL, D_FF), D_MODEL ** -0.5),
        "w_down": nrm(ks[17], (DEPTH, D_FF, D_MODEL), D_FF ** -0.5),
        "final_norm_w": 1.0 + nrm(ks[18], (D_MODEL,), 0.02),
    }


def reference(x_prompt, x_sample, cache_k, cache_v, state_conv, state_ssm,
              norm1_w, w_in, conv_w, conv_b, dt_bias, a_log, d_skip, ssm_norm_w,
              w_out, norm2_w, w_up, w_down, final_norm_w):
    hp, hs = x_prompt, x_sample
    Bp = x_prompt.shape[0]
    kp_l, vp_l, cp_l, sp_l = [], [], [], []
    ks_l, vs_l, cs_l, ss_l = [], [], [], []
    for l in range(DEPTH):
        lw = (norm1_w[l], w_in[l], conv_w[l], conv_b[l], dt_bias[l], a_log[l], d_skip[l],
              ssm_norm_w[l], w_out[l], norm2_w[l], w_up[l], w_down[l])
        conv0 = jnp.zeros((Bp, CONV_WIDTH - 1, CONV_CH), x_prompt.dtype)
        h0 = jnp.zeros((Bp, SSM_HEADS, SSM_HEAD_DIM, SSM_STATE), jnp.float32)
        hp, kp, vp, cp, sp = layer_forward(hp, None, None, conv0, h0, *lw)
        hs, kk, vv, cc, ss = layer_forward(hs, cache_k[l], cache_v[l], state_conv[l],
                                           state_ssm[l], *lw)
        kp_l.append(kp); vp_l.append(vp); cp_l.append(cp); sp_l.append(sp.astype(x_prompt.dtype))
        ks_l.append(kk.astype(cache_k.dtype)); vs_l.append(vv.astype(cache_v.dtype))
        cs_l.append(cc.astype(state_conv.dtype)); ss_l.append(ss.astype(state_ssm.dtype))
    y_prompt = rms_norm(hp, final_norm_w)
    y_sample = rms_norm(hs, final_norm_w)
    k_prompt = jnp.stack(kp_l); v_prompt = jnp.stack(vp_l)
    conv_prompt = jnp.stack(cp_l); ssm_prompt = jnp.stack(sp_l)
    k_sample = jnp.stack(ks_l); v_sample = jnp.stack(vs_l)
    conv_sample = jnp.stack(cs_l); ssm_sample = jnp.stack(ss_l)
    return (y_prompt, y_sample, k_prompt, v_prompt, conv_prompt, ssm_prompt,
            k_sample, v_sample, conv_sample, ssm_sample)
```

```python
import functools

import numpy as np
import jax
import jax.numpy as jnp
from jax import lax
from jax.experimental import pallas as pl
from jax.experimental.pallas import tpu as pltpu

F32 = jnp.float32
BF16 = jnp.bfloat16

D_MODEL = 1024
SB_HEADS = 8
SB_HEAD_DIM = 64
SB_WIDTH = SB_HEADS * SB_HEAD_DIM
SSM_HEADS = 8
SSM_HEAD_DIM = 64
SSM_WIDTH = SSM_HEADS * SSM_HEAD_DIM
SSM_GROUPS = 2
SSM_STATE = 64
CONV_WIDTH = 4
CONV_CH = SSM_WIDTH + 2 * SSM_GROUPS * SSM_STATE
D_FF = 4 * D_MODEL
EPS = 1e-5

LANES = 128
CONV_PAD = 8
DT_PAD = LANES
VMEM_LIMIT = 56 * 1024 * 1024


def _dot(a, b):
    return jnp.dot(a, b, preferred_element_type=F32)


def _dot_nt(a, b):
    return lax.dot_general(a, b, (((1,), (1,)), ((), ())), preferred_element_type=F32)


def _dot_tn(a, b):
    return lax.dot_general(a, b, (((0,), (0,)), ((), ())), preferred_element_type=F32)


def _split(x, pieces):
    out = []
    r = x
    for i in range(pieces):
        p = r.astype(BF16)
        out.append(p)
        if i + 1 < pieces:
            r = r - p.astype(F32)
    return out


def _rms(x, w):
    return x * lax.rsqrt(jnp.mean(x * x, axis=-1, keepdims=True) + EPS) * w


def _softplus(x):
    return jnp.maximum(x, 0.0) + jnp.log1p(jnp.exp(-jnp.abs(x)))


def _silu(x):
    return x / (1.0 + jnp.exp(-x))


def _const_spec(shape):
    nd = len(shape)
    return pl.BlockSpec(shape, lambda *_: (0,) * nd)


def _in_proj_kernel(x_ref, nw_ref, w_ref, q_ref, k_ref, v_ref, kb_ref, vb_ref, z_ref, xbc_ref, dt_ref):
    xn = _rms(x_ref[...], nw_ref[...]).astype(BF16)
    c = 0
    q_ref[...] = (_dot(xn, w_ref[:, c:c + SB_WIDTH]) * (SB_HEAD_DIM ** -0.5)).astype(BF16)
    c += SB_WIDTH
    k = _dot(xn, w_ref[:, c:c + SB_WIDTH])
    k_ref[...] = k
    kb_ref[...] = k.astype(BF16)
    c += SB_WIDTH
    v = _dot(xn, w_ref[:, c:c + SB_WIDTH])
    v_ref[...] = v
    vb_ref[...] = v.astype(BF16)
    c += SB_WIDTH
    z_ref[...] = _dot(xn, w_ref[:, c:c + SSM_WIDTH])
    c += SSM_WIDTH
    xbc_ref[...] = _dot(xn, w_ref[:, c:c + CONV_CH])
    c += CONV_CH
    dt_ref[...] = _dot(xn, w_ref[:, c:c + DT_PAD])


def _in_proj(x2d, norm_w, w_pad, tm):
    t = x2d.shape[0]
    n_all = w_pad.shape[1]
    row = lambda n: pl.BlockSpec((tm, n), lambda i: (i, 0))
    outs = [
        (SB_WIDTH, BF16), (SB_WIDTH, F32), (SB_WIDTH, F32), (SB_WIDTH, BF16), (SB_WIDTH, BF16),
        (SSM_WIDTH, F32), (CONV_CH, F32), (DT_PAD, F32),
    ]
    return pl.pallas_call(
        _in_proj_kernel,
        grid=(t // tm,),
        in_specs=[row(D_MODEL), _const_spec((1, D_MODEL)), _const_spec((D_MODEL, n_all))],
        out_specs=[row(n) for n, _ in outs],
        out_shape=[jax.ShapeDtypeStruct((t, n), d) for n, d in outs],
        compiler_params=pltpu.CompilerParams(
            dimension_semantics=("parallel",), vmem_limit_bytes=VMEM_LIMIT),
        name="in_proj",
    )(x2d, norm_w, w_pad)


def _cum_matrix(tk):
    j = np.arange(tk)[:, None]
    s = np.arange(tk)[None, :]
    later = -(j > s).astype(np.float32)
    total = -np.ones((tk, LANES), np.float32)
    half = np.concatenate([later, total], axis=1)
    return jnp.asarray(np.concatenate([half, half], axis=0), dtype=BF16)


def _attn_kernel(*refs, tq, tk, tkd, n_past):
    if n_past:
        q_ref, kn_ref, vn_ref, kp_ref, vp_ref, ud_ref, uo_ref, o_ref = refs
    else:
        q_ref, kn_ref, vn_ref, ud_ref, uo_ref, o_ref = refs
    qi = pl.program_id(2)
    q2 = q_ref[0]
    lane = lax.broadcasted_iota(jnp.int32, (1, LANES), 1)
    lo_lanes = lane < SB_HEAD_DIM
    qz = jnp.zeros_like(q2)
    q_heads = (jnp.where(lo_lanes, q2, qz), jnp.where(lo_lanes, qz, q2))

    def block(k_blk, v_blk, u_ref, width, visible, state):
        ca, cb, acc = state
        carries = (ca, cb)
        new_carries = []
        probs = []
        for h in range(2):
            z = _dot_nt(q_heads[h], k_blk)
            sp = jnp.maximum(z, 0.0) + jnp.log(1.0 + jnp.exp(-jnp.abs(z)))
            spm = sp if visible is None else jnp.where(visible, sp, 0.0)
            hi, lo = _split(spm, 2)
            cs = _dot(jnp.concatenate([hi, lo], axis=1), u_ref[...])
            later = cs[:, :width]
            total = cs[:, width:]
            p = jnp.exp((z - sp) + later + carries[h][:, :width])
            if visible is not None:
                p = jnp.where(visible, p, 0.0)
            probs.append(p.astype(BF16))
            new_carries.append(carries[h] + total)
        vz = jnp.zeros_like(v_blk)
        vcat = jnp.concatenate([jnp.where(lo_lanes, v_blk, vz), jnp.where(lo_lanes, vz, v_blk)], axis=0)
        acc = acc + _dot(jnp.concatenate(probs, axis=1), vcat)
        return new_carries[0], new_carries[1], acc

    zeros = jnp.zeros((tq, LANES), F32)
    state = (zeros, zeros, zeros)

    r = lax.broadcasted_iota(jnp.int32, (tq, tkd), 0)
    c = lax.broadcasted_iota(jnp.int32, (tq, tkd), 1)
    q0 = pl.multiple_of(qi * tq, tq)
    for d in reversed(range(tq // tkd)):
        k_blk = kn_ref[0, pl.ds(q0 + d * tkd, tkd), :]
        v_blk = vn_ref[0, pl.ds(q0 + d * tkd, tkd), :]
        state = block(k_blk, v_blk, ud_ref, tkd, (c + d * tkd) < r, state)

    n_off = (qi * tq) // tk

    def off_body(i, st):
        start = pl.multiple_of((n_off - 1 - i) * tk, tk)
        return block(kn_ref[0, pl.ds(start, tk), :], vn_ref[0, pl.ds(start, tk), :], uo_ref, tk, None, st)

    state = lax.fori_loop(0, n_off, off_body, state)

    if n_past:
        def past_body(i, st):
            start = pl.multiple_of((n_past - 1 - i) * tk, tk)
            return block(kp_ref[0, pl.ds(start, tk), :], vp_ref[0, pl.ds(start, tk), :], uo_ref, tk, None, st)

        state = lax.fori_loop(0, n_past, past_body, state)

    o_ref[0] = state[2].astype(BF16)


def _attention(q, kn, vn, kp, vp, tq, tk):
    b, l, _ = q.shape
    tkd = min(tk, tq)
    n_past = 0 if kp is None else kp.shape[1] // tk
    hp = SB_WIDTH // LANES
    qspec = pl.BlockSpec((1, tq, LANES), lambda bi, hi, qi: (bi, qi, hi))
    kvspec = pl.BlockSpec((1, l, LANES), lambda bi, hi, qi: (bi, 0, hi))
    ud, uo = _cum_matrix(tkd), _cum_matrix(tk)
    in_specs = [qspec, kvspec, kvspec]
    args = [q, kn, vn]
    if n_past:
        pspec = pl.BlockSpec((1, kp.shape[1], LANES), lambda bi, hi, qi: (bi, 0, hi))
        in_specs += [pspec, pspec]
        args += [kp, vp]
    in_specs += [_const_spec(ud.shape), _const_spec(uo.shape)]
    args += [ud, uo]
    return pl.pallas_call(
        functools.partial(_attn_kernel, tq=tq, tk=tk, tkd=tkd, n_past=n_past),
        grid=(b, hp, l // tq),
        in_specs=in_specs,
        out_specs=qspec,
        out_shape=jax.ShapeDtypeStruct((b, l, SB_WIDTH), BF16),
        compiler_params=pltpu.CompilerParams(
            dimension_semantics=("parallel", "parallel", "parallel"), vmem_limit_bytes=VMEM_LIMIT),
        name="sb_attn",
    )(*args)


def _ssd_kernel(xbc_ref, z_ref, dt_ref, cw_ref, cb_ref, dtb_ref, alog_ref, dskip_ref, nw_ref,
                conv0_ref, st0_ref, ltri_ref, expand_ref, eye_ref, bmask_ref,
                y_ref, convout_ref, stout_ref, xp_scr, st_scr, *, q):
    j = pl.program_id(1)

    @pl.when(j == 0)
    def _():
        st_scr[...] = st0_ref[0]
        xp_scr[CONV_PAD - (CONV_WIDTH - 1):CONV_PAD, :] = conv0_ref[0]

    xp_scr[CONV_PAD:CONV_PAD + q, :] = xbc_ref[0]
    base = CONV_PAD - (CONV_WIDTH - 1)
    conv = cb_ref[...]
    for i in range(CONV_WIDTH):
        conv = conv + xp_scr[base + i:base + i + q, :] * cw_ref[i:i + 1, :]
    tail = xp_scr[base + q:CONV_PAD + q, :]
    xp_scr[base:CONV_PAD, :] = tail
    convout_ref[0] = tail
    xact = _silu(conv)
    xs = xact[:, :SSM_WIDTH]
    b2 = xact[:, SSM_WIDTH:SSM_WIDTH + LANES].astype(BF16)
    c2 = xact[:, SSM_WIDTH + LANES:]

    dt = _softplus(dt_ref[0] + dtb_ref[...])
    a = dt * (-jnp.exp(alog_ref[...]))
    ltri = ltri_ref[...]
    acum = sum(_dot(ltri, p) for p in _split(a, 3))
    acum_p = _split(acum, 3)
    acum_t = sum(_dot_nt(eye_ref[...], p) for p in acum_p)
    expand = expand_ref[...]
    acum_e = sum(_dot(p, expand) for p in acum_p)
    dt_e = sum(_dot(p, expand) for p in _split(dt, 3))

    xdt = xs * dt_e
    xdt_b = xdt.astype(BF16)
    end_e = acum_e[q - 1:q, :]
    xw = (xdt * jnp.exp(end_e - acum_e)).astype(BF16)

    st_old = st_scr[...]
    lane = lax.broadcasted_iota(jnp.int32, (1, LANES), 1)
    lo_lanes = lane < SSM_STATE
    c2b = c2.astype(BF16)
    y = _dot(c2b, st_old.astype(BF16)) * jnp.exp(acum_e)
    st_scr[...] = st_old * jnp.exp(end_e) + bmask_ref[...] * _dot_tn(b2, xw)

    row = lax.broadcasted_iota(jnp.int32, (q, q), 0)
    col = lax.broadcasted_iota(jnp.int32, (q, q), 1)
    causal = row >= col
    cz = jnp.zeros_like(c2b)
    xz = jnp.zeros((q, LANES), BF16)
    y_pairs = []
    for g in range(SSM_GROUPS):
        cg = jnp.where(lo_lanes, c2b, cz) if g == 0 else jnp.where(lo_lanes, cz, c2b)
        gram = _dot_nt(cg, b2)
        for pair in range(2):
            ms = []
            for hh in range(2):
                h = 4 * g + 2 * pair + hh
                d = acum[:, h:h + 1] - acum_t[h:h + 1, :]
                decay = jnp.exp(jnp.where(causal, d, -1e30))
                ms.append((gram * decay).astype(BF16))
            xd = xdt_b[:, (2 * g + pair) * LANES:(2 * g + pair + 1) * LANES]
            xcat = jnp.concatenate([jnp.where(lo_lanes, xd, xz), jnp.where(lo_lanes, xz, xd)], axis=0)
            y_pairs.append(_dot(jnp.concatenate(ms, axis=1), xcat))
    y = y + jnp.concatenate(y_pairs, axis=1) + dskip_ref[...] * xs
    y = y * _silu(z_ref[0])
    gw = SSM_WIDTH // SSM_GROUPS
    outs = []
    for g in range(SSM_GROUPS):
        yg = y[:, g * gw:(g + 1) * gw]
        outs.append(yg * lax.rsqrt(jnp.mean(yg * yg, axis=-1, keepdims=True) + EPS))
    y_ref[0] = (jnp.concatenate(outs, axis=1) * nw_ref[...]).astype(BF16)

    @pl.when(j == pl.num_programs(1) - 1)
    def _():
        stout_ref[0] = st_scr[...]


def _ssd(xbc, z, dt, conv_w, conv_b, dt_bias, a_log, d_skip, ssm_norm_w, conv0, st0, q):
    b, l, _ = xbc.shape
    tri = jnp.asarray(np.tril(np.ones((q, q), np.float32)), dtype=BF16)
    expand = np.zeros((LANES, SSM_WIDTH), np.float32)
    for h in range(SSM_HEADS):
        expand[h, h * SSM_HEAD_DIM:(h + 1) * SSM_HEAD_DIM] = 1.0
    eye = jnp.asarray(np.eye(LANES, dtype=np.float32), dtype=BF16)
    pad = lambda v: jnp.pad(v.astype(F32), (0, LANES - v.shape[0])).reshape(1, LANES)
    seq = lambda n: pl.BlockSpec((1, q, n), lambda bi, ji: (bi, ji, 0))
    per_b = lambda s: pl.BlockSpec((1,) + s, lambda bi, ji: (bi, 0, 0))
    consts = [
        conv_w.astype(F32), conv_b.astype(F32).reshape(1, CONV_CH), pad(dt_bias), pad(a_log),
        jnp.repeat(d_skip.astype(F32), SSM_HEAD_DIM).reshape(1, SSM_WIDTH),
        ssm_norm_w.astype(F32).reshape(1, SSM_WIDTH),
    ]
    mats = [tri, jnp.asarray(expand, dtype=BF16), eye, _state_block_mask()]
    return pl.pallas_call(
        functools.partial(_ssd_kernel, q=q),
        grid=(b, l // q),
        in_specs=[seq(CONV_CH), seq(SSM_WIDTH), seq(DT_PAD)]
        + [_const_spec(c.shape) for c in consts]
        + [per_b((CONV_WIDTH - 1, CONV_CH)), per_b((LANES, SSM_WIDTH))]
        + [_const_spec(m.shape) for m in mats],
        out_specs=[seq(SSM_WIDTH), per_b((CONV_WIDTH - 1, CONV_CH)), per_b((LANES, SSM_WIDTH))],
        out_shape=[
            jax.ShapeDtypeStruct((b, l, SSM_WIDTH), BF16),
            jax.ShapeDtypeStruct((b, CONV_WIDTH - 1, CONV_CH), F32),
            jax.ShapeDtypeStruct((b, LANES, SSM_WIDTH), F32),
        ],
        scratch_shapes=[
            pltpu.VMEM((CONV_PAD + q, CONV_CH), F32),
            pltpu.VMEM((LANES, SSM_WIDTH), F32),
        ],
        compiler_params=pltpu.CompilerParams(
            dimension_semantics=("parallel", "arbitrary"), vmem_limit_bytes=VMEM_LIMIT),
        name="conv_ssd",
    )(xbc, z, dt, *consts, conv0, st0, *mats)


def _state_block_mask():
    m = np.zeros((LANES, SSM_WIDTH), np.float32)
    per = SSM_WIDTH // SSM_GROUPS
    for g in range(SSM_GROUPS):
        m[g * SSM_STATE:(g + 1) * SSM_STATE, g * per:(g + 1) * per] = 1.0
    return jnp.asarray(m)


def _state_to_packed(h):
    b = h.shape[0]
    hg = SSM_HEADS // SSM_GROUPS
    x = h.astype(F32).reshape(b, SSM_GROUPS, hg, SSM_HEAD_DIM, SSM_STATE)
    x = x.transpose(0, 1, 4, 2, 3).reshape(b, SSM_GROUPS, SSM_STATE, hg * SSM_HEAD_DIM)
    z = jnp.zeros_like(x)
    rows = [jnp.concatenate([x[:, g] if g2 == g else z[:, g] for g2 in range(SSM_GROUPS)], axis=-1)
            for g in range(SSM_GROUPS)]
    return jnp.concatenate(rows, axis=1)


def _state_from_packed(st):
    b = st.shape[0]
    hg = SSM_HEADS // SSM_GROUPS
    per = hg * SSM_HEAD_DIM
    blocks = [st[:, g * SSM_STATE:(g + 1) * SSM_STATE, g * per:(g + 1) * per] for g in range(SSM_GROUPS)]
    x = jnp.stack(blocks, axis=1).reshape(b, SSM_GROUPS, SSM_STATE, hg, SSM_HEAD_DIM)
    return x.transpose(0, 1, 3, 4, 2).reshape(b, SSM_HEADS, SSM_HEAD_DIM, SSM_STATE)


def _tail_kernel(x_ref, o_ref, ys_ref, woa_ref, wob_ref, n2_ref, wup_ref, wdn_ref, fn_ref, y_ref, *, ff_chunk):
    h = x_ref[...] + _dot(o_ref[...], woa_ref[...]) + _dot(ys_ref[...], wob_ref[...])
    hn = _rms(h, n2_ref[...]).astype(BF16)
    ffn = None
    for c in range(0, D_FF, ff_chunk):
        u = jnp.maximum(_dot(hn, wup_ref[:, c:c + ff_chunk]), 0.0)
        d = _dot((u * u).astype(BF16), wdn_ref[c:c + ff_chunk, :])
        ffn = d if ffn is None else ffn + d
    y_ref[...] = _rms(h + ffn, fn_ref[...])


def _tail(x2d, o2d, ys2d, wo_a, wo_b, n2, wup, wdn, fnw, tm, ff_chunk=1024):
    t = x2d.shape[0]
    row = lambda n: pl.BlockSpec((tm, n), lambda i: (i, 0))
    single = lambda s: pl.BlockSpec(s, lambda i: (0, 0), pipeline_mode=pl.Buffered(1))
    return pl.pallas_call(
        functools.partial(_tail_kernel, ff_chunk=ff_chunk),
        grid=(t // tm,),
        in_specs=[row(D_MODEL), row(SB_WIDTH), row(SSM_WIDTH),
                  single(wo_a.shape), single(wo_b.shape), single((1, D_MODEL)),
                  single(wup.shape), single(wdn.shape), single((1, D_MODEL))],
        out_specs=row(D_MODEL),
        out_shape=jax.ShapeDtypeStruct((t, D_MODEL), F32),
        compiler_params=pltpu.CompilerParams(
            dimension_semantics=("parallel",), vmem_limit_bytes=VMEM_LIMIT),
        name="out_ffn",
    )(x2d, o2d, ys2d, wo_a, wo_b, n2, wup, wdn, fnw)


def _pick(n, prefs):
    for p in prefs:
        if n % p == 0:
            return p
    raise ValueError(f"no tile in {prefs} divides {n}")


def _layer(x, k_past, v_past, conv0, st0, w):
    b, l, _ = x.shape
    t = b * l
    tm = _pick(t, (512, 256, 128, 64))
    q, k, v, kb, vb, z, xbc, dt = _in_proj(x.reshape(t, D_MODEL), w["norm1"], w["w_in"], tm)
    r3 = lambda a: a.reshape(b, l, a.shape[-1])
    tq = _pick(l, (256, 128, 64))
    o = _attention(r3(q), r3(kb), r3(vb), k_past, v_past, tq, LANES)
    ys, conv_new, st_new = _ssd(r3(xbc), r3(z), r3(dt), w["conv_w"], w["conv_b"], w["dt_bias"], w["a_log"],
                                w["d_skip"], w["ssm_norm_w"], conv0, st0, _pick(l, (128, 64)))
    y = _tail(x.reshape(t, D_MODEL), o.reshape(t, SB_WIDTH), ys.reshape(t, SSM_WIDTH),
              w["wo_a"], w["wo_b"], w["norm2"], w["w_up"], w["w_down"], w["final"], tm)
    return (y.reshape(b, l, D_MODEL), k.reshape(b, l, SB_HEADS, SB_HEAD_DIM),
            v.reshape(b, l, SB_HEADS, SB_HEAD_DIM), conv_new, _state_from_packed(st_new))


def kernel(x_prompt, x_sample, cache_k, cache_v, state_conv, state_ssm, norm1_w, w_in, conv_w, conv_b,
           dt_bias, a_log, d_skip, ssm_norm_w, w_out, norm2_w, w_up, w_down, final_norm_w):
    assert w_in.shape[0] == 1, "one layer: the final norm is fused into the layer's last kernel"
    bp = x_prompt.shape[0]
    bs, past = cache_k.shape[1], cache_k.shape[2]
    in_cols = w_in.shape[-1]
    w = {
        "norm1": norm1_w[0].reshape(1, D_MODEL),
        "w_in": jnp.pad(w_in[0], ((0, 0), (0, DT_PAD - SSM_HEADS))).astype(BF16),
        "conv_w": conv_w[0], "conv_b": conv_b[0], "dt_bias": dt_bias[0], "a_log": a_log[0],
        "d_skip": d_skip[0], "ssm_norm_w": ssm_norm_w[0],
        "wo_a": w_out[0, :SB_WIDTH].astype(BF16), "wo_b": w_out[0, SB_WIDTH:].astype(BF16),
        "norm2": norm2_w[0].reshape(1, D_MODEL),
        "w_up": w_up[0].astype(BF16), "w_down": w_down[0].astype(BF16),
        "final": final_norm_w.reshape(1, D_MODEL),
    }
    assert in_cols + DT_PAD - SSM_HEADS == 3 * SB_WIDTH + SSM_WIDTH + CONV_CH + DT_PAD
    conv_zero = jnp.zeros((bp, CONV_WIDTH - 1, CONV_CH), F32)
    st_zero = jnp.zeros((bp, LANES, SSM_WIDTH), F32)
    yp, kp, vp, cp, sp = _layer(x_prompt, None, None, conv_zero, st_zero, w)
    ys, ks, vs, cs, ss = _layer(
        x_sample, cache_k[0].reshape(bs, past, SB_WIDTH).astype(BF16),
        cache_v[0].reshape(bs, past, SB_WIDTH).astype(BF16),
        state_conv[0], _state_to_packed(state_ssm[0]), w)
    return (yp, ys, kp[None], vp[None], cp[None], sp[None], ks[None], vs[None], cs[None], ss[None])
```

```python
import functools

import numpy as np
import jax
import jax.numpy as jnp
from jax import lax
from jax.experimental import pallas as pl
from jax.experimental.pallas import tpu as pltpu

F32 = jnp.float32
BF16 = jnp.bfloat16

D_MODEL = 1024
SB_HEADS = 8
SB_HEAD_DIM = 64
SB_WIDTH = SB_HEADS * SB_HEAD_DIM
SSM_HEADS = 8
SSM_HEAD_DIM = 64
SSM_WIDTH = SSM_HEADS * SSM_HEAD_DIM
SSM_GROUPS = 2
SSM_STATE = 64
CONV_WIDTH = 4
CONV_CH = SSM_WIDTH + 2 * SSM_GROUPS * SSM_STATE
D_FF = 4 * D_MODEL
EPS = 1e-5

LANES = 128
CONV_PAD = 8
DT_PAD = LANES
VMEM_LIMIT = 56 * 1024 * 1024
EXIT_LOG = -105.0


def _dot(a, b):
    return jnp.dot(a, b, preferred_element_type=F32)


def _dot_nt(a, b):
    return lax.dot_general(a, b, (((1,), (1,)), ((), ())), preferred_element_type=F32)


def _dot_tn(a, b):
    return lax.dot_general(a, b, (((0,), (0,)), ((), ())), preferred_element_type=F32)


def _split(x, pieces):
    out = []
    r = x
    for i in range(pieces):
        p = r.astype(BF16)
        out.append(p)
        if i + 1 < pieces:
            r = r - p.astype(F32)
    return out


def _rms(x, w):
    return x * lax.rsqrt(jnp.mean(x * x, axis=-1, keepdims=True) + EPS) * w


def _softplus(x):
    return jnp.maximum(x, 0.0) + jnp.log1p(jnp.exp(-jnp.abs(x)))


def _silu(x):
    return x / (1.0 + jnp.exp(-x))


def _const_spec(shape):
    nd = len(shape)
    return pl.BlockSpec(shape, lambda *_: (0,) * nd)


def _in_proj_kernel(x_ref, nw_ref, w_ref, q_ref, k_ref, v_ref, kb_ref, vb_ref, z_ref, xbc_ref, dt_ref):
    xn = _rms(x_ref[...], nw_ref[...]).astype(BF16)
    c = 0
    q_ref[...] = (_dot(xn, w_ref[:, c:c + SB_WIDTH]) * (SB_HEAD_DIM ** -0.5)).astype(BF16)
    c += SB_WIDTH
    k = _dot(xn, w_ref[:, c:c + SB_WIDTH])
    k_ref[...] = k
    kb_ref[...] = k.astype(BF16)
    c += SB_WIDTH
    v = _dot(xn, w_ref[:, c:c + SB_WIDTH])
    v_ref[...] = v
    vb_ref[...] = v.astype(BF16)
    c += SB_WIDTH
    z_ref[...] = _dot(xn, w_ref[:, c:c + SSM_WIDTH])
    c += SSM_WIDTH
    xbc_ref[...] = _dot(xn, w_ref[:, c:c + CONV_CH])
    c += CONV_CH
    dt_ref[...] = _dot(xn, w_ref[:, c:c + DT_PAD])


def _in_proj(x2d, norm_w, w_pad, tm):
    t = x2d.shape[0]
    n_all = w_pad.shape[1]
    row = lambda n: pl.BlockSpec((tm, n), lambda i: (i, 0))
    outs = [
        (SB_WIDTH, BF16), (SB_WIDTH, F32), (SB_WIDTH, F32), (SB_WIDTH, BF16), (SB_WIDTH, BF16),
        (SSM_WIDTH, F32), (CONV_CH, F32), (DT_PAD, F32),
    ]
    return pl.pallas_call(
        _in_proj_kernel,
        grid=(t // tm,),
        in_specs=[row(D_MODEL), _const_spec((1, D_MODEL)), _const_spec((D_MODEL, n_all))],
        out_specs=[row(n) for n, _ in outs],
        out_shape=[jax.ShapeDtypeStruct((t, n), d) for n, d in outs],
        compiler_params=pltpu.CompilerParams(
            dimension_semantics=("parallel",), vmem_limit_bytes=VMEM_LIMIT),
        name="in_proj",
    )(x2d, norm_w, w_pad)


def _cum_matrix(tk):
    j = np.arange(tk)[:, None]
    s = np.arange(tk)[None, :]
    later = -(j > s).astype(np.float32)
    total = -np.ones((tk, LANES), np.float32)
    half = np.concatenate([later, total], axis=1)
    return jnp.asarray(np.concatenate([half, half], axis=0), dtype=BF16)


def _attn_kernel(*refs, tq, tk, tkd, n_past, nq):
    if n_past:
        q_ref, kn_ref, vn_ref, kp_ref, vp_ref, ud_ref, uo_ref, o_ref = refs
    else:
        q_ref, kn_ref, vn_ref, ud_ref, uo_ref, o_ref = refs
    lane = lax.broadcasted_iota(jnp.int32, (1, LANES), 1)
    lo_lanes = lane < SB_HEAD_DIM

    def block(q_heads, k_blk, v_blk, u_ref, width, visible, row0, state):
        carries = state[:2]
        new_carries = []
        probs = []
        for h in range(2):
            z = _dot_nt(q_heads[h][row0:], k_blk)
            sp = jnp.maximum(z, 0.0) + jnp.log(1.0 + jnp.exp(-jnp.abs(z)))
            spm = sp if visible is None else jnp.where(visible, sp, 0.0)
            hi, lo = _split(spm, 2)
            cs = _dot(jnp.concatenate([hi, lo], axis=1), u_ref[...])
            later = cs[:, :width]
            total = cs[:, width:]
            carry = carries[h][row0:]
            p = jnp.exp((z - sp) + later + carry[:, :width])
            if visible is not None:
                p = jnp.where(visible, p, 0.0)
            probs.append(p.astype(BF16))
            new_carries.append(carry + total)
        vz = jnp.zeros_like(v_blk)
        vcat = jnp.concatenate([jnp.where(lo_lanes, v_blk, vz), jnp.where(lo_lanes, vz, v_blk)], axis=0)
        acc = state[2][row0:] + _dot(jnp.concatenate(probs, axis=1), vcat)
        out = (new_carries[0], new_carries[1], acc)
        if row0:
            out = tuple(jnp.concatenate([old[:row0], new], axis=0) for old, new in zip(state, out))
        return out

    def live(state):
        return jnp.max(jnp.maximum(state[0], state[1])) > EXIT_LOG

    def sweep(q_heads, k_ref, v_ref, n_blocks, group, state):
        def cond(c):
            return jnp.logical_and(c[0] < n_blocks // group, c[1])

        def body(c):
            i, _, st = c
            for g in range(group):
                start = pl.multiple_of((n_blocks - 1 - (i * group + g)) * tk, tk)
                st = block(q_heads, k_ref[0, pl.ds(start, tk), :], v_ref[0, pl.ds(start, tk), :],
                           uo_ref, tk, None, 0, st)
            return i + 1, live(st), st

        return lax.while_loop(cond, body, (jnp.int32(0), live(state), state))[2]

    def tile(qi):
        q0 = qi * tq if isinstance(qi, int) else pl.multiple_of(qi * tq, tq)
        q2 = q_ref[0, pl.ds(q0, tq), :]
        qz = jnp.zeros_like(q2)
        q_heads = (jnp.where(lo_lanes, q2, qz), jnp.where(lo_lanes, qz, q2))
        zeros = jnp.zeros((tq, LANES), F32)
        state = (zeros, zeros, zeros)
        for d in reversed(range(tq // tkd)):
            rows = tq - d * tkd
            visible = (lax.broadcasted_iota(jnp.int32, (rows, tkd), 1)
                       < lax.broadcasted_iota(jnp.int32, (rows, tkd), 0))
            state = block(q_heads, kn_ref[0, pl.ds(q0 + d * tkd, tkd), :], vn_ref[0, pl.ds(q0 + d * tkd, tkd), :],
                          ud_ref, tkd, visible, d * tkd, state)
        if nq > 1:
            state = sweep(q_heads, kn_ref, vn_ref, qi * (tq // tk), tq // tk, state)
        if n_past:
            state = sweep(q_heads, kp_ref, vp_ref, n_past, 2 if n_past % 2 == 0 else 1, state)
        o_ref[0, pl.ds(q0, tq), :] = state[2].astype(BF16)

    if nq == 1:
        tile(0)
    else:
        def tile_body(qi, c):
            tile(qi)
            return c

        lax.fori_loop(0, nq, tile_body, 0)


def _attention(q, kn, vn, kp, vp, tq, tk):
    b, l, _ = q.shape
    tkd = min(tk, tq)
    nq = l // tq
    assert nq == 1 or tq % tk == 0
    n_past = 0 if kp is None else kp.shape[1] // tk
    hp = SB_WIDTH // LANES
    seq = lambda n: pl.BlockSpec((1, n, LANES), lambda bi, hi: (bi, 0, hi))
    ud, uo = _cum_matrix(tkd), _cum_matrix(tk)
    in_specs = [seq(l), seq(l), seq(l)]
    args = [q, kn, vn]
    if n_past:
        in_specs += [seq(kp.shape[1]), seq(kp.shape[1])]
        args += [kp, vp]
    in_specs += [_const_spec(ud.shape), _const_spec(uo.shape)]
    args += [ud, uo]
    return pl.pallas_call(
        functools.partial(_attn_kernel, tq=tq, tk=tk, tkd=tkd, n_past=n_past, nq=nq),
        grid=(b, hp),
        in_specs=in_specs,
        out_specs=seq(l),
        out_shape=jax.ShapeDtypeStruct((b, l, SB_WIDTH), BF16),
        compiler_params=pltpu.CompilerParams(
            dimension_semantics=("parallel", "parallel"), vmem_limit_bytes=VMEM_LIMIT),
        name="sb_attn",
    )(*args)


def _ssd_kernel(xbc_ref, z_ref, dt_ref, cw_ref, cb_ref, dtb_ref, alog_ref, dskip_ref, nw_ref,
                conv0_ref, st0_ref, ltri_ref, expand_ref, eye_ref, bmask_ref,
                y_ref, convout_ref, stout_ref, xp_scr, st_scr, *, q):
    j = pl.program_id(1)

    @pl.when(j == 0)
    def _():
        st_scr[...] = st0_ref[0]
        xp_scr[CONV_PAD - (CONV_WIDTH - 1):CONV_PAD, :] = conv0_ref[0]

    xp_scr[CONV_PAD:CONV_PAD + q, :] = xbc_ref[0]
    base = CONV_PAD - (CONV_WIDTH - 1)
    conv = cb_ref[...]
    for i in range(CONV_WIDTH):
        conv = conv + xp_scr[base + i:base + i + q, :] * cw_ref[i:i + 1, :]
    tail = xp_scr[base + q:CONV_PAD + q, :]
    xp_scr[base:CONV_PAD, :] = tail
    convout_ref[0] = tail
    xact = _silu(conv)
    xs = xact[:, :SSM_WIDTH]
    b2 = xact[:, SSM_WIDTH:SSM_WIDTH + LANES].astype(BF16)
    c2 = xact[:, SSM_WIDTH + LANES:]

    dt = _softplus(dt_ref[0] + dtb_ref[...])
    a = dt * (-jnp.exp(alog_ref[...]))
    ltri = ltri_ref[...]
    acum = sum(_dot(ltri, p) for p in _split(a, 3))
    acum_p = _split(acum, 3)
    acum_t = sum(_dot_nt(eye_ref[...], p) for p in acum_p)
    expand = expand_ref[...]
    acum_e = sum(_dot(p, expand) for p in acum_p)
    dt_e = sum(_dot(p, expand) for p in _split(dt, 3))

    xdt = xs * dt_e
    xdt_b = xdt.astype(BF16)
    end_e = acum_e[q - 1:q, :]
    xw = (xdt * jnp.exp(end_e - acum_e)).astype(BF16)

    st_old = st_scr[...]
    lane = lax.broadcasted_iota(jnp.int32, (1, LANES), 1)
    lo_lanes = lane < SSM_STATE
    c2b = c2.astype(BF16)
    y = _dot(c2b, st_old.astype(BF16)) * jnp.exp(acum_e)
    st_scr[...] = st_old * jnp.exp(end_e) + bmask_ref[...] * _dot_tn(b2, xw)

    row = lax.broadcasted_iota(jnp.int32, (q, q), 0)
    col = lax.broadcasted_iota(jnp.int32, (q, q), 1)
    causal = row >= col
    cz = jnp.zeros_like(c2b)
    xz = jnp.zeros((q, LANES), BF16)
    y_pairs = []
    for g in range(SSM_GROUPS):
        cg = jnp.where(lo_lanes, c2b, cz) if g == 0 else jnp.where(lo_lanes, cz, c2b)
        gram = _dot_nt(cg, b2)
        for pair in range(2):
            ms = []
            for hh in range(2):
                h = 4 * g + 2 * pair + hh
                d = acum[:, h:h + 1] - acum_t[h:h + 1, :]
                decay = jnp.exp(jnp.where(causal, d, -1e30))
                ms.append((gram * decay).astype(BF16))
            xd = xdt_b[:, (2 * g + pair) * LANES:(2 * g + pair + 1) * LANES]
            xcat = jnp.concatenate([jnp.where(lo_lanes, xd, xz), jnp.where(lo_lanes, xz, xd)], axis=0)
            y_pairs.append(_dot(jnp.concatenate(ms, axis=1), xcat))
    y = y + jnp.concatenate(y_pairs, axis=1) + dskip_ref[...] * xs
    y = y * _silu(z_ref[0])
    gw = SSM_WIDTH // SSM_GROUPS
    outs = []
    for g in range(SSM_GROUPS):
        yg = y[:, g * gw:(g + 1) * gw]
        outs.append(yg * lax.rsqrt(jnp.mean(yg * yg, axis=-1, keepdims=True) + EPS))
    y_ref[0] = (jnp.concatenate(outs, axis=1) * nw_ref[...]).astype(BF16)

    @pl.when(j == pl.num_programs(1) - 1)
    def _():
        stout_ref[0] = st_scr[...]


def _ssd(xbc, z, dt, conv_w, conv_b, dt_bias, a_log, d_skip, ssm_norm_w, conv0, st0, q):
    b, l, _ = xbc.shape
    tri = jnp.asarray(np.tril(np.ones((q, q), np.float32)), dtype=BF16)
    expand = np.zeros((LANES, SSM_WIDTH), np.float32)
    for h in range(SSM_HEADS):
        expand[h, h * SSM_HEAD_DIM:(h + 1) * SSM_HEAD_DIM] = 1.0
    eye = jnp.asarray(np.eye(LANES, dtype=np.float32), dtype=BF16)
    pad = lambda v: jnp.pad(v.astype(F32), (0, LANES - v.shape[0])).reshape(1, LANES)
    seq = lambda n: pl.BlockSpec((1, q, n), lambda bi, ji: (bi, ji, 0))
    per_b = lambda s: pl.BlockSpec((1,) + s, lambda bi, ji: (bi, 0, 0))
    consts = [
        conv_w.astype(F32), conv_b.astype(F32).reshape(1, CONV_CH), pad(dt_bias), pad(a_log),
        jnp.repeat(d_skip.astype(F32), SSM_HEAD_DIM).reshape(1, SSM_WIDTH),
        ssm_norm_w.astype(F32).reshape(1, SSM_WIDTH),
    ]
    mats = [tri, jnp.asarray(expand, dtype=BF16), eye, _state_block_mask()]
    return pl.pallas_call(
        functools.partial(_ssd_kernel, q=q),
        grid=(b, l // q),
        in_specs=[seq(CONV_CH), seq(SSM_WIDTH), seq(DT_PAD)]
        + [_const_spec(c.shape) for c in consts]
        + [per_b((CONV_WIDTH - 1, CONV_CH)), per_b((LANES, SSM_WIDTH))]
        + [_const_spec(m.shape) for m in mats],
        out_specs=[seq(SSM_WIDTH), per_b((CONV_WIDTH - 1, CONV_CH)), per_b((LANES, SSM_WIDTH))],
        out_shape=[
            jax.ShapeDtypeStruct((b, l, SSM_WIDTH), BF16),
            jax.ShapeDtypeStruct((b, CONV_WIDTH - 1, CONV_CH), F32),
            jax.ShapeDtypeStruct((b, LANES, SSM_WIDTH), F32),
        ],
        scratch_shapes=[
            pltpu.VMEM((CONV_PAD + q, CONV_CH), F32),
            pltpu.VMEM((LANES, SSM_WIDTH), F32),
        ],
        compiler_params=pltpu.CompilerParams(
            dimension_semantics=("parallel", "arbitrary"), vmem_limit_bytes=VMEM_LIMIT),
        name="conv_ssd",
    )(xbc, z, dt, *consts, conv0, st0, *mats)


def _state_block_mask():
    m = np.zeros((LANES, SSM_WIDTH), np.float32)
    per = SSM_WIDTH // SSM_GROUPS
    for g in range(SSM_GROUPS):
        m[g * SSM_STATE:(g + 1) * SSM_STATE, g * per:(g + 1) * per] = 1.0
    return jnp.asarray(m)


def _state_to_packed(h):
    b = h.shape[0]
    hg = SSM_HEADS // SSM_GROUPS
    x = h.astype(F32).reshape(b, SSM_GROUPS, hg, SSM_HEAD_DIM, SSM_STATE)
    x = x.transpose(0, 1, 4, 2, 3).reshape(b, SSM_GROUPS, SSM_STATE, hg * SSM_HEAD_DIM)
    z = jnp.zeros_like(x)
    rows = [jnp.concatenate([x[:, g] if g2 == g else z[:, g] for g2 in range(SSM_GROUPS)], axis=-1)
            for g in range(SSM_GROUPS)]
    return jnp.concatenate(rows, axis=1)


def _state_from_packed(st):
    b = st.shape[0]
    hg = SSM_HEADS // SSM_GROUPS
    per = hg * SSM_HEAD_DIM
    blocks = [st[:, g * SSM_STATE:(g + 1) * SSM_STATE, g * per:(g + 1) * per] for g in range(SSM_GROUPS)]
    x = jnp.stack(blocks, axis=1).reshape(b, SSM_GROUPS, SSM_STATE, hg, SSM_HEAD_DIM)
    return x.transpose(0, 1, 3, 4, 2).reshape(b, SSM_HEADS, SSM_HEAD_DIM, SSM_STATE)


def _tail_kernel(x_ref, o_ref, ys_ref, woa_ref, wob_ref, n2_ref, wup_ref, wdn_ref, fn_ref, y_ref, *, ff_chunk):
    h = x_ref[...] + _dot(o_ref[...], woa_ref[...]) + _dot(ys_ref[...], wob_ref[...])
    hn = _rms(h, n2_ref[...]).astype(BF16)
    ffn = None
    for c in range(0, D_FF, ff_chunk):
        u = jnp.maximum(_dot(hn, wup_ref[:, c:c + ff_chunk]), 0.0)
        d = _dot((u * u).astype(BF16), wdn_ref[c:c + ff_chunk, :])
        ffn = d if ffn is None else ffn + d
    y_ref[...] = _rms(h + ffn, fn_ref[...])


def _tail(x2d, o2d, ys2d, wo_a, wo_b, n2, wup, wdn, fnw, tm, ff_chunk=1024):
    t = x2d.shape[0]
    row = lambda n: pl.BlockSpec((tm, n), lambda i: (i, 0))
    single = lambda s: pl.BlockSpec(s, lambda i: (0, 0), pipeline_mode=pl.Buffered(1))
    return pl.pallas_call(
        functools.partial(_tail_kernel, ff_chunk=ff_chunk),
        grid=(t // tm,),
        in_specs=[row(D_MODEL), row(SB_WIDTH), row(SSM_WIDTH),
                  single(wo_a.shape), single(wo_b.shape), single((1, D_MODEL)),
                  single(wup.shape), single(wdn.shape), single((1, D_MODEL))],
        out_specs=row(D_MODEL),
        out_shape=jax.ShapeDtypeStruct((t, D_MODEL), F32),
        compiler_params=pltpu.CompilerParams(
            dimension_semantics=("parallel",), vmem_limit_bytes=VMEM_LIMIT),
        name="out_ffn",
    )(x2d, o2d, ys2d, wo_a, wo_b, n2, wup, wdn, fnw)


def _pick(n, prefs):
    for p in prefs:
        if n % p == 0:
            return p
    raise ValueError(f"no tile in {prefs} divides {n}")


def _layer(x, k_past, v_past, conv0, st0, w):
    b, l, _ = x.shape
    t = b * l
    tm = _pick(t, (512, 256, 128, 64))
    q, k, v, kb, vb, z, xbc, dt = _in_proj(x.reshape(t, D_MODEL), w["norm1"], w["w_in"], tm)
    r3 = lambda a: a.reshape(b, l, a.shape[-1])
    tq = _pick(l, (256, 128, 64))
    o = _attention(r3(q), r3(kb), r3(vb), k_past, v_past, tq, LANES)
    ys, conv_new, st_new = _ssd(r3(xbc), r3(z), r3(dt), w["conv_w"], w["conv_b"], w["dt_bias"], w["a_log"],
                                w["d_skip"], w["ssm_norm_w"], conv0, st0, _pick(l, (128, 64)))
    y = _tail(x.reshape(t, D_MODEL), o.reshape(t, SB_WIDTH), ys.reshape(t, SSM_WIDTH),
              w["wo_a"], w["wo_b"], w["norm2"], w["w_up"], w["w_down"], w["final"], tm)
    return (y.reshape(b, l, D_MODEL), k.reshape(b, l, SB_HEADS, SB_HEAD_DIM),
            v.reshape(b, l, SB_HEADS, SB_HEAD_DIM), conv_new, _state_from_packed(st_new))


def kernel(x_prompt, x_sample, cache_k, cache_v, state_conv, state_ssm, norm1_w, w_in, conv_w, conv_b,
           dt_bias, a_log, d_skip, ssm_norm_w, w_out, norm2_w, w_up, w_down, final_norm_w):
    assert w_in.shape[0] == 1, "one layer: the final norm is fused into the layer's last kernel"
    bp = x_prompt.shape[0]
    bs, past = cache_k.shape[1], cache_k.shape[2]
    in_cols = w_in.shape[-1]
    w = {
        "norm1": norm1_w[0].reshape(1, D_MODEL),
        "w_in": jnp.pad(w_in[0], ((0, 0), (0, DT_PAD - SSM_HEADS))).astype(BF16),
        "conv_w": conv_w[0], "conv_b": conv_b[0], "dt_bias": dt_bias[0], "a_log": a_log[0],
        "d_skip": d_skip[0], "ssm_norm_w": ssm_norm_w[0],
        "wo_a": w_out[0, :SB_WIDTH].astype(BF16), "wo_b": w_out[0, SB_WIDTH:].astype(BF16),
        "norm2": norm2_w[0].reshape(1, D_MODEL),
        "w_up": w_up[0].astype(BF16), "w_down": w_down[0].astype(BF16),
        "final": final_norm_w.reshape(1, D_MODEL),
    }
    assert in_cols + DT_PAD - SSM_HEADS == 3 * SB_WIDTH + SSM_WIDTH + CONV_CH + DT_PAD
    conv_zero = jnp.zeros((bp, CONV_WIDTH - 1, CONV_CH), F32)
    st_zero = jnp.zeros((bp, LANES, SSM_WIDTH), F32)
    yp, kp, vp, cp, sp = _layer(x_prompt, None, None, conv_zero, st_zero, w)
    ys, ks, vs, cs, ss = _layer(
        x_sample, cache_k[0].reshape(bs, past, SB_WIDTH).astype(BF16),
        cache_v[0].reshape(bs, past, SB_WIDTH).astype(BF16),
        state_conv[0], _state_to_packed(state_ssm[0]), w)
    return (yp, ys, kp[None], vp[None], cp[None], sp[None], ks[None], vs[None], cs[None], ss[None])
```

```python
import functools

import numpy as np
import jax
import jax.numpy as jnp
from jax import lax
from jax.experimental import pallas as pl
from jax.experimental.pallas import tpu as pltpu

F32 = jnp.float32
BF16 = jnp.bfloat16

D_MODEL = 1024
SB_HEADS = 8
SB_HEAD_DIM = 64
SB_WIDTH = SB_HEADS * SB_HEAD_DIM
SSM_HEADS = 8
SSM_HEAD_DIM = 64
SSM_WIDTH = SSM_HEADS * SSM_HEAD_DIM
SSM_GROUPS = 2
SSM_STATE = 64
CONV_WIDTH = 4
CONV_CH = SSM_WIDTH + 2 * SSM_GROUPS * SSM_STATE
D_FF = 4 * D_MODEL
EPS = 1e-5

LANES = 128
CONV_PAD = 8
DT_PAD = LANES
VMEM_LIMIT = 56 * 1024 * 1024
EXIT_LOG = -105.0


def _dot(a, b):
    return jnp.dot(a, b, preferred_element_type=F32)


def _dot_nt(a, b):
    return lax.dot_general(a, b, (((1,), (1,)), ((), ())), preferred_element_type=F32)


def _dot_tn(a, b):
    return lax.dot_general(a, b, (((0,), (0,)), ((), ())), preferred_element_type=F32)


def _split(x, pieces):
    out = []
    r = x
    for i in range(pieces):
        p = r.astype(BF16)
        out.append(p)
        if i + 1 < pieces:
            r = r - p.astype(F32)
    return out


def _rms(x, w):
    return x * lax.rsqrt(jnp.mean(x * x, axis=-1, keepdims=True) + EPS) * w


def _softplus(x):
    return jnp.maximum(x, 0.0) + jnp.log1p(jnp.exp(-jnp.abs(x)))


def _silu(x):
    return x / (1.0 + jnp.exp(-x))


def _const_spec(shape):
    nd = len(shape)
    return pl.BlockSpec(shape, lambda *_: (0,) * nd)


def _in_proj_kernel(x_ref, nw_ref, w_ref, q_ref, k_ref, v_ref, kb_ref, vb_ref, z_ref, xbc_ref, dt_ref):
    xn = _rms(x_ref[...], nw_ref[...]).astype(BF16)
    c = 0
    q_ref[...] = (_dot(xn, w_ref[:, c:c + SB_WIDTH]) * (SB_HEAD_DIM ** -0.5)).astype(BF16)
    c += SB_WIDTH
    k = _dot(xn, w_ref[:, c:c + SB_WIDTH])
    k_ref[...] = k
    kb_ref[...] = k.astype(BF16)
    c += SB_WIDTH
    v = _dot(xn, w_ref[:, c:c + SB_WIDTH])
    v_ref[...] = v
    vb_ref[...] = v.astype(BF16)
    c += SB_WIDTH
    z_ref[...] = _dot(xn, w_ref[:, c:c + SSM_WIDTH])
    c += SSM_WIDTH
    xbc_ref[...] = _dot(xn, w_ref[:, c:c + CONV_CH])
    c += CONV_CH
    dt_ref[...] = _dot(xn, w_ref[:, c:c + DT_PAD])


def _in_proj(x2d, norm_w, w_pad, tm):
    t = x2d.shape[0]
    n_all = w_pad.shape[1]
    row = lambda n: pl.BlockSpec((tm, n), lambda i: (i, 0))
    outs = [
        (SB_WIDTH, BF16), (SB_WIDTH, F32), (SB_WIDTH, F32), (SB_WIDTH, BF16), (SB_WIDTH, BF16),
        (SSM_WIDTH, F32), (CONV_CH, F32), (DT_PAD, F32),
    ]
    return pl.pallas_call(
        _in_proj_kernel,
        grid=(t // tm,),
        in_specs=[row(D_MODEL), _const_spec((1, D_MODEL)), _const_spec((D_MODEL, n_all))],
        out_specs=[row(n) for n, _ in outs],
        out_shape=[jax.ShapeDtypeStruct((t, n), d) for n, d in outs],
        compiler_params=pltpu.CompilerParams(
            dimension_semantics=("parallel",), vmem_limit_bytes=VMEM_LIMIT),
        name="in_proj",
    )(x2d, norm_w, w_pad)


def _cum_matrix(tk):
    j = np.arange(tk)[:, None]
    s = np.arange(tk)[None, :]
    later = -(j > s).astype(np.float32)
    total = -np.ones((tk, LANES), np.float32)
    half = np.concatenate([later, total], axis=1)
    return jnp.asarray(np.concatenate([half, half], axis=0), dtype=BF16)


def _attn_kernel(*refs, tq, tk, tkd, n_past, nq, hps):
    if n_past:
        q_ref, kn_ref, vn_ref, kp_ref, vp_ref, ud_ref, uo_ref, o_ref = refs
    else:
        q_ref, kn_ref, vn_ref, ud_ref, uo_ref, o_ref = refs
    lane = lax.broadcasted_iota(jnp.int32, (1, LANES), 1)
    lo_lanes = lane < SB_HEAD_DIM
    n_fast = 2

    def blocks(q_heads, specs, state):
        zs = [[_dot_nt(q_heads[h][r0:r1], k_blk) for h in range(2)]
              for k_blk, _, _, _, _, (r0, r1) in specs]
        mids = []
        for (_, _, u_ref, width, visible, _), zb in zip(specs, zs):
            mid = []
            for z in zb:
                sp = jnp.maximum(z, 0.0) + jnp.log(1.0 + jnp.exp(-jnp.abs(z)))
                spm = sp if visible is None else jnp.where(visible, sp, 0.0)
                hi, lo = _split(spm, 2)
                cs = _dot(jnp.concatenate([hi, lo], axis=1), u_ref[...])
                mid.append((z - sp, cs))
            mids.append(mid)
        for (_, v_blk, _, width, visible, (r0, r1)), mid in zip(specs, mids):
            new_carries = []
            probs = []
            for h, (log_beta, cs) in enumerate(mid):
                carry = state[h][r0:r1]
                p = jnp.exp(log_beta + cs[:, :width] + carry[:, :width])
                if visible is not None:
                    p = jnp.where(visible, p, 0.0)
                probs.append(p.astype(BF16))
                new_carries.append(carry + cs[:, width:])
            vz = jnp.zeros_like(v_blk)
            vcat = jnp.concatenate([jnp.where(lo_lanes, v_blk, vz), jnp.where(lo_lanes, vz, v_blk)], axis=0)
            acc = state[2][r0:r1] + _dot(jnp.concatenate(probs, axis=1), vcat)
            out = (new_carries[0], new_carries[1], acc)
            if (r0, r1) != (0, tq):
                out = tuple(jnp.concatenate(([old[:r0]] if r0 else []) + [new] + ([old[r1:]] if r1 < tq else []),
                                            axis=0) for old, new in zip(state, out))
            state = out
        return state

    def live(state, rows):
        return jnp.max(jnp.maximum(state[0][rows[0]:rows[1]], state[1][rows[0]:rows[1]])) > EXIT_LOG

    def kv_block(k_ref, v_ref, hp, start, size):
        cols = slice(hp * LANES, (hp + 1) * LANES)
        return k_ref[0, pl.ds(start, size), cols], v_ref[0, pl.ds(start, size), cols]

    def sweep(q_heads, k_ref, v_ref, hp, n_blocks, skip, group, state):
        def cond(c):
            return jnp.logical_and(c[0] < (n_blocks - skip) // group, c[1])

        def body(c):
            i, _, st = c
            specs = []
            for g in range(group):
                start = pl.multiple_of((n_blocks - 1 - skip - (i * group + g)) * tk, tk)
                specs.append(kv_block(k_ref, v_ref, hp, start, tk) + (uo_ref, tk, None, (0, tq)))
            st = blocks(q_heads, specs, st)
            return i + 1, live(st, (0, tq)), st

        return lax.while_loop(cond, body, (jnp.int32(0), live(state, (0, tq)), state))[2]

    def tile_head(hp, qi, with_new):
        q0 = qi * tq if isinstance(qi, int) else pl.multiple_of(qi * tq, tq)
        q2 = q_ref[0, pl.ds(q0, tq), hp * LANES:(hp + 1) * LANES]
        qz = jnp.zeros_like(q2)
        q_heads = (jnp.where(lo_lanes, q2, qz), jnp.where(lo_lanes, qz, q2))
        zeros = jnp.zeros((tq, LANES), F32)
        state = (zeros, zeros, zeros)
        specs = []
        for d in reversed(range(tq // tkd)):
            rows = tq - d * tkd
            visible = (lax.broadcasted_iota(jnp.int32, (rows, tkd), 1)
                       < lax.broadcasted_iota(jnp.int32, (rows, tkd), 0))
            specs.append(kv_block(kn_ref, vn_ref, hp, q0 + d * tkd, tkd) + (ud_ref, tkd, visible, (d * tkd, tq)))
        late = None
        if with_new:
            for j in range(n_fast):
                kv = kv_block(kn_ref, vn_ref, hp, pl.multiple_of(q0 - (j + 1) * tk, tk), tk)
                if j == 0 or tq == tk:
                    specs.append(kv + (uo_ref, tk, None, (0, tq)))
                else:
                    specs.append(kv + (uo_ref, tk, None, (0, tk)))
                    late = kv + (uo_ref, tk, None, (tk, tq))
        elif n_past:
            for j in range(n_fast):
                specs.append(kv_block(kp_ref, vp_ref, hp, (n_past - 1 - j) * tk, tk) + (uo_ref, tk, None, (0, tq)))
        state = blocks(q_heads, specs, state)
        if late is not None:
            state = lax.cond(live(state, (tk, tq)), lambda st: blocks(q_heads, [late], st), lambda st: st, state)
        return q0, q_heads, state

    def tile_tail(hp, qi, q0, q_heads, state, with_new):
        if with_new:
            state = sweep(q_heads, kn_ref, vn_ref, hp, qi * (tq // tk), n_fast, tq // tk, state)
        if n_past:
            state = sweep(q_heads, kp_ref, vp_ref, hp, n_past, 0 if with_new else n_fast, 2, state)
        o_ref[0, pl.ds(q0, tq), hp * LANES:(hp + 1) * LANES] = state[2].astype(BF16)

    def tiles(qi, with_new):
        heads = [tile_head(hp, qi, with_new) for hp in range(hps)]
        for hp, (q0, q_heads, state) in enumerate(heads):
            tile_tail(hp, qi, q0, q_heads, state, with_new)

    tiles(0, False)
    if nq > 1:
        def tile_body(qi, c):
            tiles(qi, True)
            return c

        lax.fori_loop(1, nq, tile_body, 0)


def _attention(q, kn, vn, kp, vp, tq, tk, hps):
    b, l, _ = q.shape
    tkd = min(tk, tq)
    nq = l // tq
    n_past = 0 if kp is None else kp.shape[1] // tk
    assert (nq == 1 or tq == 2 * tk) and (n_past == 0 or (nq == 1 and n_past % 2 == 0 and n_past >= 2))
    w = hps * LANES
    seq = lambda n: pl.BlockSpec((1, n, w), lambda bi, hi: (bi, 0, hi))
    ud, uo = _cum_matrix(tkd), _cum_matrix(tk)
    in_specs = [seq(l), seq(l), seq(l)]
    args = [q, kn, vn]
    if n_past:
        in_specs += [seq(kp.shape[1]), seq(kp.shape[1])]
        args += [kp, vp]
    in_specs += [_const_spec(ud.shape), _const_spec(uo.shape)]
    args += [ud, uo]
    return pl.pallas_call(
        functools.partial(_attn_kernel, tq=tq, tk=tk, tkd=tkd, n_past=n_past, nq=nq, hps=hps),
        grid=(b, SB_WIDTH // w),
        in_specs=in_specs,
        out_specs=seq(l),
        out_shape=jax.ShapeDtypeStruct((b, l, SB_WIDTH), BF16),
        compiler_params=pltpu.CompilerParams(
            dimension_semantics=("parallel", "parallel"), vmem_limit_bytes=VMEM_LIMIT),
        name="sb_attn",
    )(*args)


def _ssd_kernel(xbc_ref, z_ref, dt_ref, cw_ref, cb_ref, dtb_ref, alog_ref, dskip_ref, nw_ref,
                conv0_ref, st0_ref, ltri_ref, expand_ref, eye_ref, bmask_ref,
                y_ref, convout_ref, stout_ref, xp_scr, st_scr, *, q, cps):
    j = pl.program_id(1)
    rows = q * cps

    @pl.when(j == 0)
    def _():
        st_scr[...] = st0_ref[0]
        xp_scr[CONV_PAD - (CONV_WIDTH - 1):CONV_PAD, :] = conv0_ref[0]

    xp_scr[CONV_PAD:CONV_PAD + rows, :] = xbc_ref[0]
    base = CONV_PAD - (CONV_WIDTH - 1)
    conv = cb_ref[...]
    for i in range(CONV_WIDTH):
        conv = conv + xp_scr[base + i:base + i + rows, :] * cw_ref[i:i + 1, :]
    tail = xp_scr[base + rows:CONV_PAD + rows, :]
    xp_scr[base:CONV_PAD, :] = tail
    convout_ref[0] = tail
    xact = _silu(conv)
    xs_all = xact[:, :SSM_WIDTH]
    b2_all = xact[:, SSM_WIDTH:SSM_WIDTH + LANES].astype(BF16)
    c2_all = xact[:, SSM_WIDTH + LANES:].astype(BF16)
    dt_all = _softplus(dt_ref[0] + dtb_ref[...])
    a_all = dt_all * (-jnp.exp(alog_ref[...]))

    ltri = ltri_ref[...]
    expand = expand_ref[...]
    lane = lax.broadcasted_iota(jnp.int32, (1, LANES), 1)
    lo_lanes = lane < SSM_STATE
    causal = lax.broadcasted_iota(jnp.int32, (q, q), 0) >= lax.broadcasted_iota(jnp.int32, (q, q), 1)
    xz = jnp.zeros((q, LANES), BF16)
    st = st_scr[...]
    ys = []
    for c in range(cps):
        sl = slice(c * q, (c + 1) * q)
        xs, b2, c2b, dt = xs_all[sl], b2_all[sl], c2_all[sl], dt_all[sl]
        acum = sum(_dot(ltri, p) for p in _split(a_all[sl], 3))
        acum_p = _split(acum, 3)
        acum_t = sum(_dot_nt(eye_ref[...], p) for p in acum_p)
        acum_e = sum(_dot(p, expand) for p in acum_p)
        dt_e = sum(_dot(p, expand) for p in _split(dt, 3))
        xdt = xs * dt_e
        xdt_b = xdt.astype(BF16)
        end_e = acum_e[q - 1:q, :]
        xw = (xdt * jnp.exp(end_e - acum_e)).astype(BF16)
        y = _dot(c2b, st.astype(BF16)) * jnp.exp(acum_e)
        st = st * jnp.exp(end_e) + bmask_ref[...] * _dot_tn(b2, xw)
        cz = jnp.zeros_like(c2b)
        y_pairs = []
        for g in range(SSM_GROUPS):
            cg = jnp.where(lo_lanes, c2b, cz) if g == 0 else jnp.where(lo_lanes, cz, c2b)
            gram = _dot_nt(cg, b2)
            for pair in range(2):
                ms = []
                for hh in range(2):
                    h = 4 * g + 2 * pair + hh
                    d = acum[:, h:h + 1] - acum_t[h:h + 1, :]
                    decay = jnp.exp(jnp.where(causal, d, -1e30))
                    ms.append((gram * decay).astype(BF16))
                xd = xdt_b[:, (2 * g + pair) * LANES:(2 * g + pair + 1) * LANES]
                xcat = jnp.concatenate([jnp.where(lo_lanes, xd, xz), jnp.where(lo_lanes, xz, xd)], axis=0)
                y_pairs.append(_dot(jnp.concatenate(ms, axis=1), xcat))
        ys.append(y + jnp.concatenate(y_pairs, axis=1))
    st_scr[...] = st
    y = (ys[0] if cps == 1 else jnp.concatenate(ys, axis=0)) + dskip_ref[...] * xs_all
    y = y * _silu(z_ref[0])
    gw = SSM_WIDTH // SSM_GROUPS
    outs = []
    for g in range(SSM_GROUPS):
        yg = y[:, g * gw:(g + 1) * gw]
        outs.append(yg * lax.rsqrt(jnp.mean(yg * yg, axis=-1, keepdims=True) + EPS))
    y_ref[0] = (jnp.concatenate(outs, axis=1) * nw_ref[...]).astype(BF16)

    @pl.when(j == pl.num_programs(1) - 1)
    def _():
        stout_ref[0] = st


def _ssd(xbc, z, dt, conv_w, conv_b, dt_bias, a_log, d_skip, ssm_norm_w, conv0, st0, q, cps):
    b, l, _ = xbc.shape
    rows = q * cps
    tri = jnp.asarray(np.tril(np.ones((q, q), np.float32)), dtype=BF16)
    expand = np.zeros((LANES, SSM_WIDTH), np.float32)
    for h in range(SSM_HEADS):
        expand[h, h * SSM_HEAD_DIM:(h + 1) * SSM_HEAD_DIM] = 1.0
    eye = jnp.asarray(np.eye(LANES, dtype=np.float32), dtype=BF16)
    pad = lambda v: jnp.pad(v.astype(F32), (0, LANES - v.shape[0])).reshape(1, LANES)
    seq = lambda n: pl.BlockSpec((1, rows, n), lambda bi, ji: (bi, ji, 0))
    per_b = lambda s: pl.BlockSpec((1,) + s, lambda bi, ji: (bi, 0, 0))
    consts = [
        conv_w.astype(F32), conv_b.astype(F32).reshape(1, CONV_CH), pad(dt_bias), pad(a_log),
        jnp.repeat(d_skip.astype(F32), SSM_HEAD_DIM).reshape(1, SSM_WIDTH),
        ssm_norm_w.astype(F32).reshape(1, SSM_WIDTH),
    ]
    mats = [tri, jnp.asarray(expand, dtype=BF16), eye, _state_block_mask()]
    return pl.pallas_call(
        functools.partial(_ssd_kernel, q=q, cps=cps),
        grid=(b, l // rows),
        in_specs=[seq(CONV_CH), seq(SSM_WIDTH), seq(DT_PAD)]
        + [_const_spec(c.shape) for c in consts]
        + [per_b((CONV_WIDTH - 1, CONV_CH)), per_b((LANES, SSM_WIDTH))]
        + [_const_spec(m.shape) for m in mats],
        out_specs=[seq(SSM_WIDTH), per_b((CONV_WIDTH - 1, CONV_CH)), per_b((LANES, SSM_WIDTH))],
        out_shape=[
            jax.ShapeDtypeStruct((b, l, SSM_WIDTH), BF16),
            jax.ShapeDtypeStruct((b, CONV_WIDTH - 1, CONV_CH), F32),
            jax.ShapeDtypeStruct((b, LANES, SSM_WIDTH), F32),
        ],
        scratch_shapes=[
            pltpu.VMEM((CONV_PAD + rows, CONV_CH), F32),
            pltpu.VMEM((LANES, SSM_WIDTH), F32),
        ],
        compiler_params=pltpu.CompilerParams(
            dimension_semantics=("parallel", "arbitrary"), vmem_limit_bytes=VMEM_LIMIT),
        name="conv_ssd",
    )(xbc, z, dt, *consts, conv0, st0, *mats)


def _state_block_mask():
    m = np.zeros((LANES, SSM_WIDTH), np.float32)
    per = SSM_WIDTH // SSM_GROUPS
    for g in range(SSM_GROUPS):
        m[g * SSM_STATE:(g + 1) * SSM_STATE, g * per:(g + 1) * per] = 1.0
    return jnp.asarray(m)


def _state_to_packed(h):
    b = h.shape[0]
    hg = SSM_HEADS // SSM_GROUPS
    x = h.astype(F32).reshape(b, SSM_GROUPS, hg, SSM_HEAD_DIM, SSM_STATE)
    x = x.transpose(0, 1, 4, 2, 3).reshape(b, SSM_GROUPS, SSM_STATE, hg * SSM_HEAD_DIM)
    z = jnp.zeros_like(x)
    rows = [jnp.concatenate([x[:, g] if g2 == g else z[:, g] for g2 in range(SSM_GROUPS)], axis=-1)
            for g in range(SSM_GROUPS)]
    return jnp.concatenate(rows, axis=1)


def _state_from_packed(st):
    b = st.shape[0]
    hg = SSM_HEADS // SSM_GROUPS
    per = hg * SSM_HEAD_DIM
    blocks = [st[:, g * SSM_STATE:(g + 1) * SSM_STATE, g * per:(g + 1) * per] for g in range(SSM_GROUPS)]
    x = jnp.stack(blocks, axis=1).reshape(b, SSM_GROUPS, SSM_STATE, hg, SSM_HEAD_DIM)
    return x.transpose(0, 1, 3, 4, 2).reshape(b, SSM_HEADS, SSM_HEAD_DIM, SSM_STATE)


def _tail_kernel(x_ref, o_ref, ys_ref, woa_ref, wob_ref, n2_ref, wup_ref, wdn_ref, fn_ref, y_ref, *, ff_chunk):
    h = x_ref[...] + _dot(o_ref[...], woa_ref[...]) + _dot(ys_ref[...], wob_ref[...])
    hn = _rms(h, n2_ref[...]).astype(BF16)
    ffn = None
    for c in range(0, D_FF, ff_chunk):
        u = jnp.maximum(_dot(hn, wup_ref[:, c:c + ff_chunk]), 0.0)
        d = _dot((u * u).astype(BF16), wdn_ref[c:c + ff_chunk, :])
        ffn = d if ffn is None else ffn + d
    y_ref[...] = _rms(h + ffn, fn_ref[...])


def _tail(x2d, o2d, ys2d, wo_a, wo_b, n2, wup, wdn, fnw, tm, ff_chunk=1024):
    t = x2d.shape[0]
    row = lambda n: pl.BlockSpec((tm, n), lambda i: (i, 0))
    single = lambda s: pl.BlockSpec(s, lambda i: (0, 0), pipeline_mode=pl.Buffered(1))
    return pl.pallas_call(
        functools.partial(_tail_kernel, ff_chunk=ff_chunk),
        grid=(t // tm,),
        in_specs=[row(D_MODEL), row(SB_WIDTH), row(SSM_WIDTH),
                  single(wo_a.shape), single(wo_b.shape), single((1, D_MODEL)),
                  single(wup.shape), single(wdn.shape), single((1, D_MODEL))],
        out_specs=row(D_MODEL),
        out_shape=jax.ShapeDtypeStruct((t, D_MODEL), F32),
        compiler_params=pltpu.CompilerParams(
            dimension_semantics=("parallel",), vmem_limit_bytes=VMEM_LIMIT),
        name="out_ffn",
    )(x2d, o2d, ys2d, wo_a, wo_b, n2, wup, wdn, fnw)


def _pick(n, prefs):
    for p in prefs:
        if n % p == 0:
            return p
    raise ValueError(f"no tile in {prefs} divides {n}")


def _layer(x, k_past, v_past, conv0, st0, w):
    b, l, _ = x.shape
    t = b * l
    tm = _pick(t, (512, 256, 128, 64))
    q, k, v, kb, vb, z, xbc, dt = _in_proj(x.reshape(t, D_MODEL), w["norm1"], w["w_in"], tm)
    r3 = lambda a: a.reshape(b, l, a.shape[-1])
    tq = _pick(l, (256, 128, 64))
    o = _attention(r3(q), r3(kb), r3(vb), k_past, v_past, tq, LANES, 1 if k_past is None else SB_WIDTH // LANES)
    q_ssd = _pick(l, (128, 64))
    ys, conv_new, st_new = _ssd(r3(xbc), r3(z), r3(dt), w["conv_w"], w["conv_b"], w["dt_bias"], w["a_log"],
                                w["d_skip"], w["ssm_norm_w"], conv0, st0, q_ssd, 2 if l % (2 * q_ssd) == 0 else 1)
    y = _tail(x.reshape(t, D_MODEL), o.reshape(t, SB_WIDTH), ys.reshape(t, SSM_WIDTH),
              w["wo_a"], w["wo_b"], w["norm2"], w["w_up"], w["w_down"], w["final"], tm)
    return (y.reshape(b, l, D_MODEL), k.reshape(b, l, SB_HEADS, SB_HEAD_DIM),
            v.reshape(b, l, SB_HEADS, SB_HEAD_DIM), conv_new, _state_from_packed(st_new))


def kernel(x_prompt, x_sample, cache_k, cache_v, state_conv, state_ssm, norm1_w, w_in, conv_w, conv_b,
           dt_bias, a_log, d_skip, ssm_norm_w, w_out, norm2_w, w_up, w_down, final_norm_w):
    assert w_in.shape[0] == 1, "one layer: the final norm is fused into the layer's last kernel"
    bp = x_prompt.shape[0]
    bs, past = cache_k.shape[1], cache_k.shape[2]
    in_cols = w_in.shape[-1]
    w = {
        "norm1": norm1_w[0].reshape(1, D_MODEL),
        "w_in": jnp.pad(w_in[0], ((0, 0), (0, DT_PAD - SSM_HEADS))).astype(BF16),
        "conv_w": conv_w[0], "conv_b": conv_b[0], "dt_bias": dt_bias[0], "a_log": a_log[0],
        "d_skip": d_skip[0], "ssm_norm_w": ssm_norm_w[0],
        "wo_a": w_out[0, :SB_WIDTH].astype(BF16), "wo_b": w_out[0, SB_WIDTH:].astype(BF16),
        "norm2": norm2_w[0].reshape(1, D_MODEL),
        "w_up": w_up[0].astype(BF16), "w_down": w_down[0].astype(BF16),
        "final": final_norm_w.reshape(1, D_MODEL),
    }
    assert in_cols + DT_PAD - SSM_HEADS == 3 * SB_WIDTH + SSM_WIDTH + CONV_CH + DT_PAD
    conv_zero = jnp.zeros((bp, CONV_WIDTH - 1, CONV_CH), F32)
    st_zero = jnp.zeros((bp, LANES, SSM_WIDTH), F32)
    yp, kp, vp, cp, sp = _layer(x_prompt, None, None, conv_zero, st_zero, w)
    ys, ks, vs, cs, ss = _layer(
        x_sample, cache_k[0].reshape(bs, past, SB_WIDTH).astype(BF16),
        cache_v[0].reshape(bs, past, SB_WIDTH).astype(BF16),
        state_conv[0], _state_to_packed(state_ssm[0]), w)
    return (yp, ys, kp[None], vp[None], cp[None], sp[None], ks[None], vs[None], cs[None], ss[None])
```

```python
import functools

import numpy as np
import jax
import jax.numpy as jnp
from jax import lax
from jax.experimental import pallas as pl
from jax.experimental.pallas import tpu as pltpu

F32 = jnp.float32
BF16 = jnp.bfloat16

D_MODEL = 1024
SB_HEADS = 8
SB_HEAD_DIM = 64
SB_WIDTH = SB_HEADS * SB_HEAD_DIM
SSM_HEADS = 8
SSM_HEAD_DIM = 64
SSM_WIDTH = SSM_HEADS * SSM_HEAD_DIM
SSM_GROUPS = 2
SSM_STATE = 64
CONV_WIDTH = 4
CONV_CH = SSM_WIDTH + 2 * SSM_GROUPS * SSM_STATE
D_FF = 4 * D_MODEL
EPS = 1e-5

LANES = 128
CONV_PAD = 8
DT_PAD = LANES
VMEM_LIMIT = 56 * 1024 * 1024
EXIT_LOG = -105.0


def _dot(a, b):
    return jnp.dot(a, b, preferred_element_type=F32)


def _dot_nt(a, b):
    return lax.dot_general(a, b, (((1,), (1,)), ((), ())), preferred_element_type=F32)


def _dot_tn(a, b):
    return lax.dot_general(a, b, (((0,), (0,)), ((), ())), preferred_element_type=F32)


def _split(x, pieces):
    out = []
    r = x
    for i in range(pieces):
        p = r.astype(BF16)
        out.append(p)
        if i + 1 < pieces:
            r = r - p.astype(F32)
    return out


def _rms(x, w):
    return x * lax.rsqrt(jnp.mean(x * x, axis=-1, keepdims=True) + EPS) * w


def _softplus(x):
    return jnp.maximum(x, 0.0) + jnp.log1p(jnp.exp(-jnp.abs(x)))


def _silu(x):
    return x / (1.0 + jnp.exp(-x))


def _const_spec(shape):
    nd = len(shape)
    return pl.BlockSpec(shape, lambda *_: (0,) * nd)


def _in_proj_kernel(x_ref, nw_ref, w_ref, q_ref, k_ref, v_ref, kb_ref, vb_ref, z_ref, xbc_ref, dt_ref):
    xn = _rms(x_ref[...], nw_ref[...]).astype(BF16)
    c = 0
    q_ref[...] = (_dot(xn, w_ref[:, c:c + SB_WIDTH]) * (SB_HEAD_DIM ** -0.5)).astype(BF16)
    c += SB_WIDTH
    k = _dot(xn, w_ref[:, c:c + SB_WIDTH])
    k_ref[...] = pltpu.einshape("t(hd)->thd", k, h=SB_HEADS)
    kb_ref[...] = k.astype(BF16)
    c += SB_WIDTH
    v = _dot(xn, w_ref[:, c:c + SB_WIDTH])
    v_ref[...] = pltpu.einshape("t(hd)->thd", v, h=SB_HEADS)
    vb_ref[...] = v.astype(BF16)
    c += SB_WIDTH
    z_ref[...] = _dot(xn, w_ref[:, c:c + SSM_WIDTH])
    c += SSM_WIDTH
    xbc_ref[...] = _dot(xn, w_ref[:, c:c + CONV_CH])
    c += CONV_CH
    dt_ref[...] = _dot(xn, w_ref[:, c:c + DT_PAD])


def _in_proj(x2d, norm_w, w_pad, tm):
    t = x2d.shape[0]
    n_all = w_pad.shape[1]
    row = lambda n: pl.BlockSpec((tm, n), lambda i: (i, 0))
    heads = (SB_HEADS, SB_HEAD_DIM)
    outs = [
        ((SB_WIDTH,), BF16), (heads, F32), (heads, F32), ((SB_WIDTH,), BF16), ((SB_WIDTH,), BF16),
        ((SSM_WIDTH,), F32), ((CONV_CH,), F32), ((DT_PAD,), F32),
    ]
    return pl.pallas_call(
        _in_proj_kernel,
        grid=(t // tm,),
        in_specs=[row(D_MODEL), _const_spec((1, D_MODEL)), _const_spec((D_MODEL, n_all))],
        out_specs=[pl.BlockSpec((tm,) + n, lambda i, nd=len(n): (i,) + (0,) * nd) for n, _ in outs],
        out_shape=[jax.ShapeDtypeStruct((t,) + n, d) for n, d in outs],
        compiler_params=pltpu.CompilerParams(
            dimension_semantics=("parallel",), vmem_limit_bytes=VMEM_LIMIT),
        name="in_proj",
    )(x2d, norm_w, w_pad)


def _cum_matrix(tk):
    j = np.arange(tk)[:, None]
    s = np.arange(tk)[None, :]
    later = -(j > s).astype(np.float32)
    total = -np.ones((tk, LANES), np.float32)
    half = np.concatenate([later, total], axis=1)
    return jnp.asarray(np.concatenate([half, half], axis=0), dtype=BF16)


def _attn_kernel(*refs, tq, tk, tkd, n_past, nq, hps):
    if n_past:
        q_ref, kn_ref, vn_ref, kp_ref, vp_ref, ud_ref, uo_ref, o_ref = refs
    else:
        q_ref, kn_ref, vn_ref, ud_ref, uo_ref, o_ref = refs
    lane = lax.broadcasted_iota(jnp.int32, (1, LANES), 1)
    lo_lanes = lane < SB_HEAD_DIM
    n_fast = 2
    flat_cache = {}

    def blocks(q_heads, specs, state):
        zs = [[_dot_nt(q_heads[h][r0:r1], k_blk) for h in range(2)]
              for k_blk, _, _, _, _, (r0, r1) in specs]
        mids = []
        for (_, _, u_ref, width, visible, _), zb in zip(specs, zs):
            mid = []
            for z in zb:
                sp = jnp.maximum(z, 0.0) + jnp.log(1.0 + jnp.exp(-jnp.abs(z)))
                spm = sp if visible is None else jnp.where(visible, sp, 0.0)
                hi, lo = _split(spm, 2)
                cs = _dot(jnp.concatenate([hi, lo], axis=1), u_ref[...])
                mid.append((z - sp, cs))
            mids.append(mid)
        for (_, v_blk, _, width, visible, (r0, r1)), mid in zip(specs, mids):
            new_carries = []
            probs = []
            for h, (log_beta, cs) in enumerate(mid):
                carry = state[h][r0:r1]
                p = jnp.exp(log_beta + cs[:, :width] + carry[:, :width])
                if visible is not None:
                    p = jnp.where(visible, p, 0.0)
                probs.append(p.astype(BF16))
                new_carries.append(carry + cs[:, width:])
            vz = jnp.zeros_like(v_blk)
            vcat = jnp.concatenate([jnp.where(lo_lanes, v_blk, vz), jnp.where(lo_lanes, vz, v_blk)], axis=0)
            acc = state[2][r0:r1] + _dot(jnp.concatenate(probs, axis=1), vcat)
            out = (new_carries[0], new_carries[1], acc)
            if (r0, r1) != (0, tq):
                out = tuple(jnp.concatenate(([old[:r0]] if r0 else []) + [new] + ([old[r1:]] if r1 < tq else []),
                                            axis=0) for old, new in zip(state, out))
            state = out
        return state

    def live(state, rows):
        return jnp.max(jnp.maximum(state[0][rows[0]:rows[1]], state[1][rows[0]:rows[1]])) > EXIT_LOG

    def kv_block(k_ref, v_ref, hp, start, size):
        cols = slice(hp * LANES, (hp + 1) * LANES)
        if len(k_ref.shape) == 3:
            return k_ref[0, pl.ds(start, size), cols], v_ref[0, pl.ds(start, size), cols]
        key = start if isinstance(start, int) else None
        if key is None or key not in flat_cache:
            kv = tuple(pltpu.einshape("thd->t(hd)", r[0, pl.ds(start, size)]).astype(BF16) for r in (k_ref, v_ref))
            if key is None:
                return kv[0][:, cols], kv[1][:, cols]
            flat_cache[key] = kv
        kv = flat_cache[key]
        return kv[0][:, cols], kv[1][:, cols]

    def sweep(q_heads, k_ref, v_ref, hp, n_blocks, skip, group, state):
        def cond(c):
            return jnp.logical_and(c[0] < (n_blocks - skip) // group, c[1])

        def body(c):
            i, _, st = c
            specs = []
            for g in range(group):
                start = pl.multiple_of((n_blocks - 1 - skip - (i * group + g)) * tk, tk)
                specs.append(kv_block(k_ref, v_ref, hp, start, tk) + (uo_ref, tk, None, (0, tq)))
            st = blocks(q_heads, specs, st)
            return i + 1, live(st, (0, tq)), st

        return lax.while_loop(cond, body, (jnp.int32(0), live(state, (0, tq)), state))[2]

    def tile_head(hp, qi, with_new):
        q0 = qi * tq if isinstance(qi, int) else pl.multiple_of(qi * tq, tq)
        q2 = q_ref[0, pl.ds(q0, tq), hp * LANES:(hp + 1) * LANES]
        qz = jnp.zeros_like(q2)
        q_heads = (jnp.where(lo_lanes, q2, qz), jnp.where(lo_lanes, qz, q2))
        zeros = jnp.zeros((tq, LANES), F32)
        state = (zeros, zeros, zeros)
        specs = []
        for d in reversed(range(tq // tkd)):
            rows = tq - d * tkd
            visible = (lax.broadcasted_iota(jnp.int32, (rows, tkd), 1)
                       < lax.broadcasted_iota(jnp.int32, (rows, tkd), 0))
            specs.append(kv_block(kn_ref, vn_ref, hp, q0 + d * tkd, tkd) + (ud_ref, tkd, visible, (d * tkd, tq)))
        late = None
        if with_new:
            for j in range(n_fast):
                kv = kv_block(kn_ref, vn_ref, hp, pl.multiple_of(q0 - (j + 1) * tk, tk), tk)
                if j == 0 or tq == tk:
                    specs.append(kv + (uo_ref, tk, None, (0, tq)))
                else:
                    specs.append(kv + (uo_ref, tk, None, (0, tk)))
                    late = kv + (uo_ref, tk, None, (tk, tq))
        elif n_past:
            for j in range(n_fast):
                specs.append(kv_block(kp_ref, vp_ref, hp, (n_past - 1 - j) * tk, tk) + (uo_ref, tk, None, (0, tq)))
        state = blocks(q_heads, specs, state)
        if late is not None:
            state = lax.cond(live(state, (tk, tq)), lambda st: blocks(q_heads, [late], st), lambda st: st, state)
        return q0, q_heads, state

    def tile_tail(hp, qi, q0, q_heads, state, with_new):
        if with_new:
            state = sweep(q_heads, kn_ref, vn_ref, hp, qi * (tq // tk), n_fast, tq // tk, state)
        if n_past:
            state = sweep(q_heads, kp_ref, vp_ref, hp, n_past, 0 if with_new else n_fast, 2, state)
        o_ref[0, pl.ds(q0, tq), hp * LANES:(hp + 1) * LANES] = state[2].astype(BF16)

    def tiles(qi, with_new):
        heads = [tile_head(hp, qi, with_new) for hp in range(hps)]
        for hp, (q0, q_heads, state) in enumerate(heads):
            tile_tail(hp, qi, q0, q_heads, state, with_new)

    tiles(0, False)
    if nq > 1:
        def tile_body(qi, c):
            tiles(qi, True)
            return c

        lax.fori_loop(1, nq, tile_body, 0)


def _attention(q, kn, vn, kp, vp, tq, tk, hps):
    b, l, _ = q.shape
    tkd = min(tk, tq)
    nq = l // tq
    n_past = 0 if kp is None else kp.shape[1] // tk
    assert (nq == 1 or tq == 2 * tk) and (n_past == 0 or (nq == 1 and n_past % 2 == 0 and n_past >= 2))
    w = hps * LANES
    seq = lambda n: pl.BlockSpec((1, n, w), lambda bi, hi: (bi, 0, hi))
    ud, uo = _cum_matrix(tkd), _cum_matrix(tk)
    in_specs = [seq(l), seq(l), seq(l)]
    args = [q, kn, vn]
    if n_past:
        assert w == SB_WIDTH
        past = pl.BlockSpec((1,) + kp.shape[1:], lambda bi, hi: (bi, 0, 0, 0))
        in_specs += [past, past]
        args += [kp, vp]
    in_specs += [_const_spec(ud.shape), _const_spec(uo.shape)]
    args += [ud, uo]
    return pl.pallas_call(
        functools.partial(_attn_kernel, tq=tq, tk=tk, tkd=tkd, n_past=n_past, nq=nq, hps=hps),
        grid=(b, SB_WIDTH // w),
        in_specs=in_specs,
        out_specs=seq(l),
        out_shape=jax.ShapeDtypeStruct((b, l, SB_WIDTH), BF16),
        compiler_params=pltpu.CompilerParams(
            dimension_semantics=("parallel", "parallel"), vmem_limit_bytes=VMEM_LIMIT),
        name="sb_attn",
    )(*args)


def _ssd_kernel(xbc_ref, z_ref, dt_ref, cw_ref, cb_ref, dtb_ref, alog_ref, dskip_ref, nw_ref,
                conv0_ref, st0_ref, ltri_ref, expand_ref, eye_ref, bmask_ref,
                y_ref, convout_ref, stout_ref, xp_scr, st_scr, *, q, cps):
    j = pl.program_id(1)
    rows = q * cps

    @pl.when(j == 0)
    def _():
        st_scr[...] = st0_ref[0]
        xp_scr[CONV_PAD - (CONV_WIDTH - 1):CONV_PAD, :] = conv0_ref[0]

    xp_scr[CONV_PAD:CONV_PAD + rows, :] = xbc_ref[0]
    base = CONV_PAD - (CONV_WIDTH - 1)
    conv = cb_ref[...]
    for i in range(CONV_WIDTH):
        conv = conv + xp_scr[base + i:base + i + rows, :] * cw_ref[i:i + 1, :]
    tail = xp_scr[base + rows:CONV_PAD + rows, :]
    xp_scr[base:CONV_PAD, :] = tail
    convout_ref[0] = tail
    xact = _silu(conv)
    xs_all = xact[:, :SSM_WIDTH]
    b2_all = xact[:, SSM_WIDTH:SSM_WIDTH + LANES].astype(BF16)
    c2_all = xact[:, SSM_WIDTH + LANES:].astype(BF16)
    dt_all = _softplus(dt_ref[0] + dtb_ref[...])
    a_all = dt_all * (-jnp.exp(alog_ref[...]))

    ltri = ltri_ref[...]
    expand = expand_ref[...]
    lane = lax.broadcasted_iota(jnp.int32, (1, LANES), 1)
    lo_lanes = lane < SSM_STATE
    causal = lax.broadcasted_iota(jnp.int32, (q, q), 0) >= lax.broadcasted_iota(jnp.int32, (q, q), 1)
    xz = jnp.zeros((q, LANES), BF16)
    st = st_scr[...]
    ys = []
    for c in range(cps):
        sl = slice(c * q, (c + 1) * q)
        xs, b2, c2b, dt = xs_all[sl], b2_all[sl], c2_all[sl], dt_all[sl]
        acum = sum(_dot(ltri, p) for p in _split(a_all[sl], 3))
        acum_p = _split(acum, 3)
        acum_t = sum(_dot_nt(eye_ref[...], p) for p in acum_p)
        acum_e = sum(_dot(p, expand) for p in acum_p)
        dt_e = sum(_dot(p, expand) for p in _split(dt, 3))
        xdt = xs * dt_e
        xdt_b = xdt.astype(BF16)
        end_e = acum_e[q - 1:q, :]
        xw = (xdt * jnp.exp(end_e - acum_e)).astype(BF16)
        y = _dot(c2b, st.astype(BF16)) * jnp.exp(acum_e)
        st = st * jnp.exp(end_e) + bmask_ref[...] * _dot_tn(b2, xw)
        cz = jnp.zeros_like(c2b)
        y_pairs = []
        for g in range(SSM_GROUPS):
            cg = jnp.where(lo_lanes, c2b, cz) if g == 0 else jnp.where(lo_lanes, cz, c2b)
            gram = _dot_nt(cg, b2)
            for pair in range(2):
                ms = []
                for hh in range(2):
                    h = 4 * g + 2 * pair + hh
                    d = acum[:, h:h + 1] - acum_t[h:h + 1, :]
                    decay = jnp.exp(jnp.where(causal, d, -1e30))
                    ms.append((gram * decay).astype(BF16))
                xd = xdt_b[:, (2 * g + pair) * LANES:(2 * g + pair + 1) * LANES]
                xcat = jnp.concatenate([jnp.where(lo_lanes, xd, xz), jnp.where(lo_lanes, xz, xd)], axis=0)
                y_pairs.append(_dot(jnp.concatenate(ms, axis=1), xcat))
        ys.append(y + jnp.concatenate(y_pairs, axis=1))
    st_scr[...] = st
    y = (ys[0] if cps == 1 else jnp.concatenate(ys, axis=0)) + dskip_ref[...] * xs_all
    y = y * _silu(z_ref[0])
    gw = SSM_WIDTH // SSM_GROUPS
    outs = []
    for g in range(SSM_GROUPS):
        yg = y[:, g * gw:(g + 1) * gw]
        outs.append(yg * lax.rsqrt(jnp.mean(yg * yg, axis=-1, keepdims=True) + EPS))
    y_ref[0] = (jnp.concatenate(outs, axis=1) * nw_ref[...]).astype(BF16)

    @pl.when(j == pl.num_programs(1) - 1)
    def _():
        stout_ref[0] = st


def _ssd(xbc, z, dt, conv_w, conv_b, dt_bias, a_log, d_skip, ssm_norm_w, conv0, st0, q, cps):
    b, l, _ = xbc.shape
    rows = q * cps
    tri = jnp.asarray(np.tril(np.ones((q, q), np.float32)), dtype=BF16)
    expand = np.zeros((LANES, SSM_WIDTH), np.float32)
    for h in range(SSM_HEADS):
        expand[h, h * SSM_HEAD_DIM:(h + 1) * SSM_HEAD_DIM] = 1.0
    eye = jnp.asarray(np.eye(LANES, dtype=np.float32), dtype=BF16)
    pad = lambda v: jnp.pad(v.astype(F32), (0, LANES - v.shape[0])).reshape(1, LANES)
    seq = lambda n: pl.BlockSpec((1, rows, n), lambda bi, ji: (bi, ji, 0))
    per_b = lambda s: pl.BlockSpec((1,) + s, lambda bi, ji: (bi, 0, 0))
    consts = [
        conv_w.astype(F32), conv_b.astype(F32).reshape(1, CONV_CH), pad(dt_bias), pad(a_log),
        jnp.repeat(d_skip.astype(F32), SSM_HEAD_DIM).reshape(1, SSM_WIDTH),
        ssm_norm_w.astype(F32).reshape(1, SSM_WIDTH),
    ]
    mats = [tri, jnp.asarray(expand, dtype=BF16), eye, _state_block_mask()]
    return pl.pallas_call(
        functools.partial(_ssd_kernel, q=q, cps=cps),
        grid=(b, l // rows),
        in_specs=[seq(CONV_CH), seq(SSM_WIDTH), seq(DT_PAD)]
        + [_const_spec(c.shape) for c in consts]
        + [per_b((CONV_WIDTH - 1, CONV_CH)), per_b((LANES, SSM_WIDTH))]
        + [_const_spec(m.shape) for m in mats],
        out_specs=[seq(SSM_WIDTH), per_b((CONV_WIDTH - 1, CONV_CH)), per_b((LANES, SSM_WIDTH))],
        out_shape=[
            jax.ShapeDtypeStruct((b, l, SSM_WIDTH), BF16),
            jax.ShapeDtypeStruct((b, CONV_WIDTH - 1, CONV_CH), F32),
            jax.ShapeDtypeStruct((b, LANES, SSM_WIDTH), F32),
        ],
        scratch_shapes=[
            pltpu.VMEM((CONV_PAD + rows, CONV_CH), F32),
            pltpu.VMEM((LANES, SSM_WIDTH), F32),
        ],
        compiler_params=pltpu.CompilerParams(
            dimension_semantics=("parallel", "arbitrary"), vmem_limit_bytes=VMEM_LIMIT),
        name="conv_ssd",
    )(xbc, z, dt, *consts, conv0, st0, *mats)


def _state_block_mask():
    m = np.zeros((LANES, SSM_WIDTH), np.float32)
    per = SSM_WIDTH // SSM_GROUPS
    for g in range(SSM_GROUPS):
        m[g * SSM_STATE:(g + 1) * SSM_STATE, g * per:(g + 1) * per] = 1.0
    return jnp.asarray(m)


def _state_to_packed(h):
    b = h.shape[0]
    hg = SSM_HEADS // SSM_GROUPS
    x = h.astype(F32).reshape(b, SSM_GROUPS, hg, SSM_HEAD_DIM, SSM_STATE)
    x = x.transpose(0, 1, 4, 2, 3).reshape(b, SSM_GROUPS, SSM_STATE, hg * SSM_HEAD_DIM)
    z = jnp.zeros_like(x)
    rows = [jnp.concatenate([x[:, g] if g2 == g else z[:, g] for g2 in range(SSM_GROUPS)], axis=-1)
            for g in range(SSM_GROUPS)]
    return jnp.concatenate(rows, axis=1)


def _state_from_packed(st):
    b = st.shape[0]
    hg = SSM_HEADS // SSM_GROUPS
    per = hg * SSM_HEAD_DIM
    blocks = [st[:, g * SSM_STATE:(g + 1) * SSM_STATE, g * per:(g + 1) * per] for g in range(SSM_GROUPS)]
    x = jnp.stack(blocks, axis=1).reshape(b, SSM_GROUPS, SSM_STATE, hg, SSM_HEAD_DIM)
    return x.transpose(0, 1, 3, 4, 2).reshape(b, SSM_HEADS, SSM_HEAD_DIM, SSM_STATE)


def _tail_kernel(x_ref, o_ref, ys_ref, woa_ref, wob_ref, n2_ref, wup_ref, wdn_ref, fn_ref, y_ref, *, ff_chunk):
    h = x_ref[...] + _dot(o_ref[...], woa_ref[...]) + _dot(ys_ref[...], wob_ref[...])
    hn = _rms(h, n2_ref[...]).astype(BF16)
    ffn = None
    for c in range(0, D_FF, ff_chunk):
        u = jnp.maximum(_dot(hn, wup_ref[:, c:c + ff_chunk]), 0.0)
        d = _dot((u * u).astype(BF16), wdn_ref[c:c + ff_chunk, :])
        ffn = d if ffn is None else ffn + d
    y_ref[...] = _rms(h + ffn, fn_ref[...])


def _tail(x2d, o2d, ys2d, wo_a, wo_b, n2, wup, wdn, fnw, tm, ff_chunk=1024):
    t = x2d.shape[0]
    row = lambda n: pl.BlockSpec((tm, n), lambda i: (i, 0))
    single = lambda s: pl.BlockSpec(s, lambda i: (0, 0), pipeline_mode=pl.Buffered(1))
    return pl.pallas_call(
        functools.partial(_tail_kernel, ff_chunk=ff_chunk),
        grid=(t // tm,),
        in_specs=[row(D_MODEL), row(SB_WIDTH), row(SSM_WIDTH),
                  single(wo_a.shape), single(wo_b.shape), single((1, D_MODEL)),
                  single(wup.shape), single(wdn.shape), single((1, D_MODEL))],
        out_specs=row(D_MODEL),
        out_shape=jax.ShapeDtypeStruct((t, D_MODEL), F32),
        compiler_params=pltpu.CompilerParams(
            dimension_semantics=("parallel",), vmem_limit_bytes=VMEM_LIMIT),
        name="out_ffn",
    )(x2d, o2d, ys2d, wo_a, wo_b, n2, wup, wdn, fnw)


def _pick(n, prefs):
    for p in prefs:
        if n % p == 0:
            return p
    raise ValueError(f"no tile in {prefs} divides {n}")


def _layer(x, k_past, v_past, conv0, st0, w):
    b, l, _ = x.shape
    t = b * l
    tm = _pick(t, (512, 256, 128, 64))
    q, k, v, kb, vb, z, xbc, dt = _in_proj(x.reshape(t, D_MODEL), w["norm1"], w["w_in"], tm)
    r3 = lambda a: a.reshape(b, l, a.shape[-1])
    tq = _pick(l, (256, 128, 64))
    o = _attention(r3(q), r3(kb), r3(vb), k_past, v_past, tq, LANES, 1 if k_past is None else SB_WIDTH // LANES)
    q_ssd = _pick(l, (128, 64))
    ys, conv_new, st_new = _ssd(r3(xbc), r3(z), r3(dt), w["conv_w"], w["conv_b"], w["dt_bias"], w["a_log"],
                                w["d_skip"], w["ssm_norm_w"], conv0, st0, q_ssd, 2 if l % (2 * q_ssd) == 0 else 1)
    y = _tail(x.reshape(t, D_MODEL), o.reshape(t, SB_WIDTH), ys.reshape(t, SSM_WIDTH),
              w["wo_a"], w["wo_b"], w["norm2"], w["w_up"], w["w_down"], w["final"], tm)
    return (y.reshape(b, l, D_MODEL), k.reshape(b, l, SB_HEADS, SB_HEAD_DIM),
            v.reshape(b, l, SB_HEADS, SB_HEAD_DIM), conv_new, _state_from_packed(st_new))


def kernel(x_prompt, x_sample, cache_k, cache_v, state_conv, state_ssm, norm1_w, w_in, conv_w, conv_b,
           dt_bias, a_log, d_skip, ssm_norm_w, w_out, norm2_w, w_up, w_down, final_norm_w):
    assert w_in.shape[0] == 1, "one layer: the final norm is fused into the layer's last kernel"
    bp = x_prompt.shape[0]
    in_cols = w_in.shape[-1]
    w = {
        "norm1": norm1_w[0].reshape(1, D_MODEL),
        "w_in": jnp.pad(w_in[0], ((0, 0), (0, DT_PAD - SSM_HEADS))).astype(BF16),
        "conv_w": conv_w[0], "conv_b": conv_b[0], "dt_bias": dt_bias[0], "a_log": a_log[0],
        "d_skip": d_skip[0], "ssm_norm_w": ssm_norm_w[0],
        "wo_a": w_out[0, :SB_WIDTH].astype(BF16), "wo_b": w_out[0, SB_WIDTH:].astype(BF16),
        "norm2": norm2_w[0].reshape(1, D_MODEL),
        "w_up": w_up[0].astype(BF16), "w_down": w_down[0].astype(BF16),
        "final": final_norm_w.reshape(1, D_MODEL),
    }
    assert in_cols + DT_PAD - SSM_HEADS == 3 * SB_WIDTH + SSM_WIDTH + CONV_CH + DT_PAD
    conv_zero = jnp.zeros((bp, CONV_WIDTH - 1, CONV_CH), F32)
    st_zero = jnp.zeros((bp, LANES, SSM_WIDTH), F32)
    yp, kp, vp, cp, sp = _layer(x_prompt, None, None, conv_zero, st_zero, w)
    ys, ks, vs, cs, ss = _layer(x_sample, cache_k[0], cache_v[0], state_conv[0], _state_to_packed(state_ssm[0]), w)
    return (yp, ys, kp[None], vp[None], cp[None], sp[None], ks[None], vs[None], cs[None], ss[None])
```

```python
import functools

import numpy as np
import jax
import jax.numpy as jnp
from jax import lax
from jax.experimental import pallas as pl
from jax.experimental.pallas import tpu as pltpu

F32 = jnp.float32
BF16 = jnp.bfloat16

D_MODEL = 1024
SB_HEADS = 8
SB_HEAD_DIM = 64
SB_WIDTH = SB_HEADS * SB_HEAD_DIM
SSM_HEADS = 8
SSM_HEAD_DIM = 64
SSM_WIDTH = SSM_HEADS * SSM_HEAD_DIM
SSM_GROUPS = 2
SSM_STATE = 64
CONV_WIDTH = 4
CONV_CH = SSM_WIDTH + 2 * SSM_GROUPS * SSM_STATE
D_FF = 4 * D_MODEL
EPS = 1e-5

LANES = 128
CONV_PAD = 8
DT_PAD = LANES
VMEM_LIMIT = 56 * 1024 * 1024
EXIT_LOG = -105.0


def _dot(a, b):
    return jnp.dot(a, b, preferred_element_type=F32)


def _dot_nt(a, b):
    return lax.dot_general(a, b, (((1,), (1,)), ((), ())), preferred_element_type=F32)


def _dot_tn(a, b):
    return lax.dot_general(a, b, (((0,), (0,)), ((), ())), preferred_element_type=F32)


def _split(x, pieces):
    out = []
    r = x
    for i in range(pieces):
        p = r.astype(BF16)
        out.append(p)
        if i + 1 < pieces:
            r = r - p.astype(F32)
    return out


def _rms(x, w):
    return x * lax.rsqrt(jnp.mean(x * x, axis=-1, keepdims=True) + EPS) * w


def _softplus(x):
    return jnp.maximum(x, 0.0) + jnp.log1p(jnp.exp(-jnp.abs(x)))


def _silu(x):
    return x / (1.0 + jnp.exp(-x))


def _const_spec(shape):
    nd = len(shape)
    return pl.BlockSpec(shape, lambda *_: (0,) * nd)


def _in_proj_kernel(x_ref, nw_ref, w_ref, wkv_ref, q_ref, kt_ref, vt_ref, ktb_ref, vtb_ref, z_ref, xbc_ref, dt_ref):
    xn = _rms(x_ref[0], nw_ref[...]).astype(BF16)
    tm = xn.shape[0]
    c = 0
    q_ref[0] = (_dot(xn, w_ref[:, c:c + SB_WIDTH]) * (SB_HEAD_DIM ** -0.5)).astype(BF16)
    c += SB_WIDTH
    for r, (t_ref, tb_ref) in enumerate(((kt_ref, ktb_ref), (vt_ref, vtb_ref))):
        t = _dot_nt(wkv_ref[r * SB_WIDTH:(r + 1) * SB_WIDTH, :], xn)
        t_ref[0] = t.reshape(SB_HEADS, SB_HEAD_DIM, tm)
        tb_ref[0] = t.astype(BF16)
    z_ref[0] = _dot(xn, w_ref[:, c:c + SSM_WIDTH])
    c += SSM_WIDTH
    xbc_ref[0] = _dot(xn, w_ref[:, c:c + CONV_CH])
    c += CONV_CH
    dt_ref[0] = _dot(xn, w_ref[:, c:c + DT_PAD])


def _in_proj(x, norm_w, w_rest, w_kv_t, tm):
    b, l, _ = x.shape
    row = lambda n: pl.BlockSpec((1, tm, n), lambda i, j: (i, j, 0))
    outs = [
        ((l, SB_WIDTH), BF16, row(SB_WIDTH)),
        ((SB_HEADS, SB_HEAD_DIM, l), F32, pl.BlockSpec((1, SB_HEADS, SB_HEAD_DIM, tm), lambda i, j: (i, 0, 0, j))),
        ((SB_HEADS, SB_HEAD_DIM, l), F32, pl.BlockSpec((1, SB_HEADS, SB_HEAD_DIM, tm), lambda i, j: (i, 0, 0, j))),
        ((SB_WIDTH, l), BF16, pl.BlockSpec((1, SB_WIDTH, tm), lambda i, j: (i, 0, j))),
        ((SB_WIDTH, l), BF16, pl.BlockSpec((1, SB_WIDTH, tm), lambda i, j: (i, 0, j))),
        ((l, SSM_WIDTH), F32, row(SSM_WIDTH)), ((l, CONV_CH), F32, row(CONV_CH)), ((l, DT_PAD), F32, row(DT_PAD)),
    ]
    return pl.pallas_call(
        _in_proj_kernel,
        grid=(b, l // tm),
        in_specs=[row(D_MODEL), _const_spec((1, D_MODEL)), _const_spec(w_rest.shape), _const_spec(w_kv_t.shape)],
        out_specs=[spec for _, _, spec in outs],
        out_shape=[jax.ShapeDtypeStruct((b,) + shape, d) for shape, d, _ in outs],
        compiler_params=pltpu.CompilerParams(
            dimension_semantics=("parallel", "parallel"), vmem_limit_bytes=VMEM_LIMIT),
        name="in_proj",
    )(x, norm_w, w_rest, w_kv_t)


def _cum_matrix(tk):
    j = np.arange(tk)[:, None]
    s = np.arange(tk)[None, :]
    later = -(j > s).astype(np.float32)
    total = -np.ones((tk, LANES), np.float32)
    half = np.concatenate([later, total], axis=1)
    return jnp.asarray(np.concatenate([half, half], axis=0), dtype=BF16)


def _attn_kernel(*refs, tq, tk, tkd, n_past, nq, hps):
    if n_past:
        q_ref, kn_ref, vn_ref, kp_ref, vp_ref, ud_ref, uo_ref, o_ref = refs
    else:
        q_ref, kn_ref, vn_ref, ud_ref, uo_ref, o_ref = refs
    lane = lax.broadcasted_iota(jnp.int32, (1, LANES), 1)
    lo_lanes = lane < SB_HEAD_DIM
    lo_rows = lax.broadcasted_iota(jnp.int32, (LANES, 1), 0) < SB_HEAD_DIM
    n_fast = 2

    def blocks(q_heads, specs, state):
        zs = [[_dot(q_heads[h][r0:r1], kt_blk) for h in range(2)]
              for kt_blk, _, _, _, _, (r0, r1) in specs]
        mids = []
        for (_, _, u_ref, width, visible, _), zb in zip(specs, zs):
            mid = []
            for z in zb:
                sp = jnp.maximum(z, 0.0) + jnp.log(1.0 + jnp.exp(-jnp.abs(z)))
                spm = sp if visible is None else jnp.where(visible, sp, 0.0)
                hi, lo = _split(spm, 2)
                cs = _dot(jnp.concatenate([hi, lo], axis=1), u_ref[...])
                mid.append((z - sp, cs))
            mids.append(mid)
        for (_, vt_blk, _, width, visible, (r0, r1)), mid in zip(specs, mids):
            new_carries = []
            probs = []
            for h, (log_beta, cs) in enumerate(mid):
                carry = state[h][r0:r1]
                p = jnp.exp(log_beta + cs[:, :width] + carry[:, :width])
                if visible is not None:
                    p = jnp.where(visible, p, 0.0)
                probs.append(p.astype(BF16))
                new_carries.append(carry + cs[:, width:])
            vz = jnp.zeros_like(vt_blk)
            vcat = jnp.concatenate([jnp.where(lo_rows, vt_blk, vz), jnp.where(lo_rows, vz, vt_blk)], axis=1)
            acc = state[2][r0:r1] + _dot_nt(jnp.concatenate(probs, axis=1), vcat)
            out = (new_carries[0], new_carries[1], acc)
            if (r0, r1) != (0, tq):
                out = tuple(jnp.concatenate(([old[:r0]] if r0 else []) + [new] + ([old[r1:]] if r1 < tq else []),
                                            axis=0) for old, new in zip(state, out))
            state = out
        return state

    def live(state, rows):
        return jnp.max(jnp.maximum(state[0][rows[0]:rows[1]], state[1][rows[0]:rows[1]])) > EXIT_LOG

    def kv_block(k_ref, v_ref, hp, start, size):
        if len(k_ref.shape) == 3:
            rows = slice(hp * LANES, (hp + 1) * LANES)
            return k_ref[0, rows, pl.ds(start, size)], v_ref[0, rows, pl.ds(start, size)]
        return tuple(r[0, 2 * hp:2 * hp + 2, :, pl.ds(start, size)].reshape(LANES, size).astype(BF16)
                     for r in (k_ref, v_ref))

    def sweep(q_heads, k_ref, v_ref, hp, n_blocks, skip, group, state):
        def cond(c):
            return jnp.logical_and(c[0] < (n_blocks - skip) // group, c[1])

        def body(c):
            i, _, st = c
            specs = []
            for g in range(group):
                start = pl.multiple_of((n_blocks - 1 - skip - (i * group + g)) * tk, tk)
                specs.append(kv_block(k_ref, v_ref, hp, start, tk) + (uo_ref, tk, None, (0, tq)))
            st = blocks(q_heads, specs, st)
            return i + 1, live(st, (0, tq)), st

        return lax.while_loop(cond, body, (jnp.int32(0), live(state, (0, tq)), state))[2]

    def tile_head(hp, qi, with_new):
        q0 = qi * tq if isinstance(qi, int) else pl.multiple_of(qi * tq, tq)
        q2 = q_ref[0, pl.ds(q0, tq), hp * LANES:(hp + 1) * LANES]
        qz = jnp.zeros_like(q2)
        q_heads = (jnp.where(lo_lanes, q2, qz), jnp.where(lo_lanes, qz, q2))
        zeros = jnp.zeros((tq, LANES), F32)
        state = (zeros, zeros, zeros)
        specs = []
        for d in reversed(range(tq // tkd)):
            rows = tq - d * tkd
            visible = (lax.broadcasted_iota(jnp.int32, (rows, tkd), 1)
                       < lax.broadcasted_iota(jnp.int32, (rows, tkd), 0))
            specs.append(kv_block(kn_ref, vn_ref, hp, q0 + d * tkd, tkd) + (ud_ref, tkd, visible, (d * tkd, tq)))
        late = None
        if with_new:
            for j in range(n_fast):
                kv = kv_block(kn_ref, vn_ref, hp, pl.multiple_of(q0 - (j + 1) * tk, tk), tk)
                if j == 0 or tq == tk:
                    specs.append(kv + (uo_ref, tk, None, (0, tq)))
                else:
                    specs.append(kv + (uo_ref, tk, None, (0, tk)))
                    late = kv + (uo_ref, tk, None, (tk, tq))
        elif n_past:
            for j in range(n_fast):
                specs.append(kv_block(kp_ref, vp_ref, hp, (n_past - 1 - j) * tk, tk) + (uo_ref, tk, None, (0, tq)))
        state = blocks(q_heads, specs, state)
        if late is not None:
            state = lax.cond(live(state, (tk, tq)), lambda st: blocks(q_heads, [late], st), lambda st: st, state)
        return q0, q_heads, state

    def tile_tail(hp, qi, q0, q_heads, state, with_new):
        if with_new:
            state = sweep(q_heads, kn_ref, vn_ref, hp, qi * (tq // tk), n_fast, tq // tk, state)
        if n_past:
            state = sweep(q_heads, kp_ref, vp_ref, hp, n_past, 0 if with_new else n_fast, 2, state)
        o_ref[0, pl.ds(q0, tq), hp * LANES:(hp + 1) * LANES] = state[2].astype(BF16)

    def tiles(qi, with_new):
        heads = [tile_head(hp, qi, with_new) for hp in range(hps)]
        for hp, (q0, q_heads, state) in enumerate(heads):
            tile_tail(hp, qi, q0, q_heads, state, with_new)

    tiles(0, False)
    if nq > 1:
        def tile_body(qi, c):
            tiles(qi, True)
            return c

        lax.fori_loop(1, nq, tile_body, 0)


def _attention(q, kn, vn, kp, vp, tq, tk, hps):
    b, l, _ = q.shape
    tkd = min(tk, tq)
    nq = l // tq
    n_past = 0 if kp is None else kp.shape[3] // tk
    assert (nq == 1 or tq == 2 * tk) and (n_past == 0 or (nq == 1 and n_past % 2 == 0 and n_past >= 2))
    w = hps * LANES
    seq = lambda n: pl.BlockSpec((1, n, w), lambda bi, hi: (bi, 0, hi))
    seq_t = pl.BlockSpec((1, w, l), lambda bi, hi: (bi, hi, 0))
    ud, uo = _cum_matrix(tkd), _cum_matrix(tk)
    in_specs = [seq(l), seq_t, seq_t]
    args = [q, kn, vn]
    if n_past:
        assert w == SB_WIDTH
        past = pl.BlockSpec((1,) + kp.shape[1:], lambda bi, hi: (bi, 0, 0, 0))
        in_specs += [past, past]
        args += [kp, vp]
    in_specs += [_const_spec(ud.shape), _const_spec(uo.shape)]
    args += [ud, uo]
    return pl.pallas_call(
        functools.partial(_attn_kernel, tq=tq, tk=tk, tkd=tkd, n_past=n_past, nq=nq, hps=hps),
        grid=(b, SB_WIDTH // w),
        in_specs=in_specs,
        out_specs=seq(l),
        out_shape=jax.ShapeDtypeStruct((b, l, SB_WIDTH), BF16),
        compiler_params=pltpu.CompilerParams(
            dimension_semantics=("parallel", "parallel"), vmem_limit_bytes=VMEM_LIMIT),
        name="sb_attn",
    )(*args)


def _ssd_kernel(xbc_ref, z_ref, dt_ref, cw_ref, cb_ref, dtb_ref, alog_ref, dskip_ref, nw_ref,
                conv0_ref, st0_ref, ltri_ref, expand_ref, eye_ref, bmask_ref,
                y_ref, convout_ref, stout_ref, xp_scr, st_scr, *, q, cps):
    j = pl.program_id(1)
    rows = q * cps

    @pl.when(j == 0)
    def _():
        st_scr[...] = st0_ref[0]
        xp_scr[CONV_PAD - (CONV_WIDTH - 1):CONV_PAD, :] = conv0_ref[0]

    xp_scr[CONV_PAD:CONV_PAD + rows, :] = xbc_ref[0]
    base = CONV_PAD - (CONV_WIDTH - 1)
    conv = cb_ref[...]
    for i in range(CONV_WIDTH):
        conv = conv + xp_scr[base + i:base + i + rows, :] * cw_ref[i:i + 1, :]
    tail = xp_scr[base + rows:CONV_PAD + rows, :]
    xp_scr[base:CONV_PAD, :] = tail
    convout_ref[0] = tail
    xact = _silu(conv)
    xs_all = xact[:, :SSM_WIDTH]
    b2_all = xact[:, SSM_WIDTH:SSM_WIDTH + LANES].astype(BF16)
    c2_all = xact[:, SSM_WIDTH + LANES:].astype(BF16)
    dt_all = _softplus(dt_ref[0] + dtb_ref[...])
    a_all = dt_all * (-jnp.exp(alog_ref[...]))

    ltri = ltri_ref[...]
    expand = expand_ref[...]
    lane = lax.broadcasted_iota(jnp.int32, (1, LANES), 1)
    lo_lanes = lane < SSM_STATE
    causal = lax.broadcasted_iota(jnp.int32, (q, q), 0) >= lax.broadcasted_iota(jnp.int32, (q, q), 1)
    xz = jnp.zeros((q, LANES), BF16)
    st = st_scr[...]
    ys = []
    for c in range(cps):
        sl = slice(c * q, (c + 1) * q)
        xs, b2, c2b, dt = xs_all[sl], b2_all[sl], c2_all[sl], dt_all[sl]
        acum = sum(_dot(ltri, p) for p in _split(a_all[sl], 3))
        acum_p = _split(acum, 3)
        acum_t = sum(_dot_nt(eye_ref[...], p) for p in acum_p)
        acum_e = sum(_dot(p, expand) for p in acum_p)
        dt_e = sum(_dot(p, expand) for p in _split(dt, 3))
        xdt = xs * dt_e
        xdt_b = xdt.astype(BF16)
        end_e = acum_e[q - 1:q, :]
        xw = (xdt * jnp.exp(end_e - acum_e)).astype(BF16)
        y = _dot(c2b, st.astype(BF16)) * jnp.exp(acum_e)
        st = st * jnp.exp(end_e) + bmask_ref[...] * _dot_tn(b2, xw)
        cz = jnp.zeros_like(c2b)
        y_pairs = []
        for g in range(SSM_GROUPS):
            cg = jnp.where(lo_lanes, c2b, cz) if g == 0 else jnp.where(lo_lanes, cz, c2b)
            gram = _dot_nt(cg, b2)
            for pair in range(2):
                ms = []
                for hh in range(2):
                    h = 4 * g + 2 * pair + hh
                    d = acum[:, h:h + 1] - acum_t[h:h + 1, :]
                    decay = jnp.exp(jnp.where(causal, d, -1e30))
                    ms.append((gram * decay).astype(BF16))
                xd = xdt_b[:, (2 * g + pair) * LANES:(2 * g + pair + 1) * LANES]
                xcat = jnp.concatenate([jnp.where(lo_lanes, xd, xz), jnp.where(lo_lanes, xz, xd)], axis=0)
                y_pairs.append(_dot(jnp.concatenate(ms, axis=1), xcat))
        ys.append(y + jnp.concatenate(y_pairs, axis=1))
    st_scr[...] = st
    y = (ys[0] if cps == 1 else jnp.concatenate(ys, axis=0)) + dskip_ref[...] * xs_all
    y = y * _silu(z_ref[0])
    gw = SSM_WIDTH // SSM_GROUPS
    outs = []
    for g in range(SSM_GROUPS):
        yg = y[:, g * gw:(g + 1) * gw]
        outs.append(yg * lax.rsqrt(jnp.mean(yg * yg, axis=-1, keepdims=True) + EPS))
    y_ref[0] = (jnp.concatenate(outs, axis=1) * nw_ref[...]).astype(BF16)

    @pl.when(j == pl.num_programs(1) - 1)
    def _():
        stout_ref[0] = st


def _ssd(xbc, z, dt, conv_w, conv_b, dt_bias, a_log, d_skip, ssm_norm_w, conv0, st0, q, cps):
    b, l, _ = xbc.shape
    rows = q * cps
    tri = jnp.asarray(np.tril(np.ones((q, q), np.float32)), dtype=BF16)
    expand = np.zeros((LANES, SSM_WIDTH), np.float32)
    for h in range(SSM_HEADS):
        expand[h, h * SSM_HEAD_DIM:(h + 1) * SSM_HEAD_DIM] = 1.0
    eye = jnp.asarray(np.eye(LANES, dtype=np.float32), dtype=BF16)
    pad = lambda v: jnp.pad(v.astype(F32), (0, LANES - v.shape[0])).reshape(1, LANES)
    seq = lambda n: pl.BlockSpec((1, rows, n), lambda bi, ji: (bi, ji, 0))
    per_b = lambda s: pl.BlockSpec((1,) + s, lambda bi, ji: (bi, 0, 0))
    consts = [
        conv_w.astype(F32), conv_b.astype(F32).reshape(1, CONV_CH), pad(dt_bias), pad(a_log),
        jnp.repeat(d_skip.astype(F32), SSM_HEAD_DIM).reshape(1, SSM_WIDTH),
        ssm_norm_w.astype(F32).reshape(1, SSM_WIDTH),
    ]
    mats = [tri, jnp.asarray(expand, dtype=BF16), eye, _state_block_mask()]
    return pl.pallas_call(
        functools.partial(_ssd_kernel, q=q, cps=cps),
        grid=(b, l // rows),
        in_specs=[seq(CONV_CH), seq(SSM_WIDTH), seq(DT_PAD)]
        + [_const_spec(c.shape) for c in consts]
        + [per_b((CONV_WIDTH - 1, CONV_CH)), per_b((LANES, SSM_WIDTH))]
        + [_const_spec(m.shape) for m in mats],
        out_specs=[seq(SSM_WIDTH), per_b((CONV_WIDTH - 1, CONV_CH)), per_b((LANES, SSM_WIDTH))],
        out_shape=[
            jax.ShapeDtypeStruct((b, l, SSM_WIDTH), BF16),
            jax.ShapeDtypeStruct((b, CONV_WIDTH - 1, CONV_CH), F32),
            jax.ShapeDtypeStruct((b, LANES, SSM_WIDTH), F32),
        ],
        scratch_shapes=[
            pltpu.VMEM((CONV_PAD + rows, CONV_CH), F32),
            pltpu.VMEM((LANES, SSM_WIDTH), F32),
        ],
        compiler_params=pltpu.CompilerParams(
            dimension_semantics=("parallel", "arbitrary"), vmem_limit_bytes=VMEM_LIMIT),
        name="conv_ssd",
    )(xbc, z, dt, *consts, conv0, st0, *mats)


def _state_block_mask():
    m = np.zeros((LANES, SSM_WIDTH), np.float32)
    per = SSM_WIDTH // SSM_GROUPS
    for g in range(SSM_GROUPS):
        m[g * SSM_STATE:(g + 1) * SSM_STATE, g * per:(g + 1) * per] = 1.0
    return jnp.asarray(m)


def _state_to_packed(h):
    b = h.shape[0]
    hg = SSM_HEADS // SSM_GROUPS
    x = h.astype(F32).reshape(b, SSM_GROUPS, hg, SSM_HEAD_DIM, SSM_STATE)
    x = x.transpose(0, 1, 4, 2, 3).reshape(b, SSM_GROUPS, SSM_STATE, hg * SSM_HEAD_DIM)
    z = jnp.zeros_like(x)
    rows = [jnp.concatenate([x[:, g] if g2 == g else z[:, g] for g2 in range(SSM_GROUPS)], axis=-1)
            for g in range(SSM_GROUPS)]
    return jnp.concatenate(rows, axis=1)


def _state_from_packed(st):
    b = st.shape[0]
    hg = SSM_HEADS // SSM_GROUPS
    per = hg * SSM_HEAD_DIM
    blocks = [st[:, g * SSM_STATE:(g + 1) * SSM_STATE, g * per:(g + 1) * per] for g in range(SSM_GROUPS)]
    x = jnp.stack(blocks, axis=1).reshape(b, SSM_GROUPS, SSM_STATE, hg, SSM_HEAD_DIM)
    return x.transpose(0, 1, 3, 4, 2).reshape(b, SSM_HEADS, SSM_HEAD_DIM, SSM_STATE)


def _tail_kernel(x_ref, o_ref, ys_ref, woa_ref, wob_ref, n2_ref, wup_ref, wdn_ref, fn_ref, y_ref, *, ff_chunk):
    h = x_ref[...] + _dot(o_ref[...], woa_ref[...]) + _dot(ys_ref[...], wob_ref[...])
    hn = _rms(h, n2_ref[...]).astype(BF16)
    ffn = None
    for c in range(0, D_FF, ff_chunk):
        u = jnp.maximum(_dot(hn, wup_ref[:, c:c + ff_chunk]), 0.0)
        d = _dot((u * u).astype(BF16), wdn_ref[c:c + ff_chunk, :])
        ffn = d if ffn is None else ffn + d
    y_ref[...] = _rms(h + ffn, fn_ref[...])


def _tail(x2d, o2d, ys2d, wo_a, wo_b, n2, wup, wdn, fnw, tm, ff_chunk=1024):
    t = x2d.shape[0]
    row = lambda n: pl.BlockSpec((tm, n), lambda i: (i, 0))
    single = lambda s: pl.BlockSpec(s, lambda i: (0, 0), pipeline_mode=pl.Buffered(1))
    return pl.pallas_call(
        functools.partial(_tail_kernel, ff_chunk=ff_chunk),
        grid=(t // tm,),
        in_specs=[row(D_MODEL), row(SB_WIDTH), row(SSM_WIDTH),
                  single(wo_a.shape), single(wo_b.shape), single((1, D_MODEL)),
                  single(wup.shape), single(wdn.shape), single((1, D_MODEL))],
        out_specs=row(D_MODEL),
        out_shape=jax.ShapeDtypeStruct((t, D_MODEL), F32),
        compiler_params=pltpu.CompilerParams(
            dimension_semantics=("parallel",), vmem_limit_bytes=VMEM_LIMIT),
        name="out_ffn",
    )(x2d, o2d, ys2d, wo_a, wo_b, n2, wup, wdn, fnw)


def _pick(n, prefs):
    for p in prefs:
        if n % p == 0:
            return p
    raise ValueError(f"no tile in {prefs} divides {n}")


def _layer(x, k_past, v_past, conv0, st0, w):
    b, l, _ = x.shape
    t = b * l
    tm = _pick(t, (512, 256, 128, 64))
    if l % tm == 0:
        q, kt, vt, ktb, vtb, z, xbc, dt = _in_proj(x, w["norm1"], w["w_rest"], w["w_kv_t"], tm)
    else:
        outs = _in_proj(x.reshape(1, t, D_MODEL), w["norm1"], w["w_rest"], w["w_kv_t"], tm)
        split_t = lambda a: jnp.moveaxis(a.reshape(a.shape[1:-1] + (b, l)), -2, 0)
        q, z, xbc, dt = (a.reshape(b, l, a.shape[-1]) for a in (outs[0], outs[5], outs[6], outs[7]))
        kt, vt, ktb, vtb = (split_t(a) for a in outs[1:5])
    tq = _pick(l, (256, 128, 64))
    o = _attention(q, ktb, vtb, k_past, v_past, tq, LANES, 1 if k_past is None else SB_WIDTH // LANES)
    q_ssd = _pick(l, (128, 64))
    ys, conv_new, st_new = _ssd(xbc, z, dt, w["conv_w"], w["conv_b"], w["dt_bias"], w["a_log"],
                                w["d_skip"], w["ssm_norm_w"], conv0, st0, q_ssd, 2 if l % (2 * q_ssd) == 0 else 1)
    y = _tail(x.reshape(t, D_MODEL), o.reshape(t, SB_WIDTH), ys.reshape(t, SSM_WIDTH),
              w["wo_a"], w["wo_b"], w["norm2"], w["w_up"], w["w_down"], w["final"], tm)
    return y.reshape(b, l, D_MODEL), kt, vt, conv_new, _state_from_packed(st_new)


def kernel(x_prompt, x_sample, cache_k, cache_v, state_conv, state_ssm, norm1_w, w_in, conv_w, conv_b,
           dt_bias, a_log, d_skip, ssm_norm_w, w_out, norm2_w, w_up, w_down, final_norm_w):
    assert w_in.shape[0] == 1, "one layer: the final norm is fused into the layer's last kernel"
    bp = x_prompt.shape[0]
    in_cols = w_in.shape[-1]
    w = {
        "norm1": norm1_w[0].reshape(1, D_MODEL),
        "w_rest": jnp.pad(jnp.concatenate([w_in[0][:, :SB_WIDTH], w_in[0][:, 3 * SB_WIDTH:]], axis=1),
                          ((0, 0), (0, DT_PAD - SSM_HEADS))).astype(BF16),
        "w_kv_t": w_in[0][:, SB_WIDTH:3 * SB_WIDTH].T.astype(BF16),
        "conv_w": conv_w[0], "conv_b": conv_b[0], "dt_bias": dt_bias[0], "a_log": a_log[0],
        "d_skip": d_skip[0], "ssm_norm_w": ssm_norm_w[0],
        "wo_a": w_out[0, :SB_WIDTH].astype(BF16), "wo_b": w_out[0, SB_WIDTH:].astype(BF16),
        "norm2": norm2_w[0].reshape(1, D_MODEL),
        "w_up": w_up[0].astype(BF16), "w_down": w_down[0].astype(BF16),
        "final": final_norm_w.reshape(1, D_MODEL),
    }
    assert in_cols == 3 * SB_WIDTH + SSM_WIDTH + CONV_CH + SSM_HEADS
    conv_zero = jnp.zeros((bp, CONV_WIDTH - 1, CONV_CH), F32)
    st_zero = jnp.zeros((bp, LANES, SSM_WIDTH), F32)
    yp, kp, vp, cp, sp = _layer(x_prompt, None, None, conv_zero, st_zero, w)
    to_t = lambda a: jnp.transpose(a, (0, 2, 3, 1))
    from_t = lambda a: jnp.transpose(a, (0, 3, 1, 2))[None]
    ys, ks, vs, cs, ss = _layer(x_sample, to_t(cache_k[0]), to_t(cache_v[0]), state_conv[0],
                                _state_to_packed(state_ssm[0]), w)
    return (yp, ys, from_t(kp), from_t(vp), cp[None], sp[None], from_t(ks), from_t(vs), cs[None], ss[None])
```

```python
import functools

import numpy as np
import jax
import jax.numpy as jnp
from jax import lax
from jax.experimental import pallas as pl
from jax.experimental.pallas import tpu as pltpu

F32 = jnp.float32
BF16 = jnp.bfloat16

D_MODEL = 1024
SB_HEADS = 8
SB_HEAD_DIM = 64
SB_WIDTH = SB_HEADS * SB_HEAD_DIM
SSM_HEADS = 8
SSM_HEAD_DIM = 64
SSM_WIDTH = SSM_HEADS * SSM_HEAD_DIM
SSM_GROUPS = 2
SSM_STATE = 64
CONV_WIDTH = 4
CONV_CH = SSM_WIDTH + 2 * SSM_GROUPS * SSM_STATE
D_FF = 4 * D_MODEL
EPS = 1e-5

LANES = 128
CONV_PAD = 8
DT_PAD = LANES
VMEM_LIMIT = 56 * 1024 * 1024
EXIT_LOG = -105.0


def _dot(a, b):
    return jnp.dot(a, b, preferred_element_type=F32)


def _dot_nt(a, b):
    return lax.dot_general(a, b, (((1,), (1,)), ((), ())), preferred_element_type=F32)


def _dot_tn(a, b):
    return lax.dot_general(a, b, (((0,), (0,)), ((), ())), preferred_element_type=F32)


def _split(x, pieces):
    out = []
    r = x
    for i in range(pieces):
        p = r.astype(BF16)
        out.append(p)
        if i + 1 < pieces:
            r = r - p.astype(F32)
    return out


def _rms(x, w):
    return x * lax.rsqrt(jnp.mean(x * x, axis=-1, keepdims=True) + EPS) * w


def _softplus(x):
    return jnp.maximum(x, 0.0) + jnp.log1p(jnp.exp(-jnp.abs(x)))


def _silu(x):
    return x / (1.0 + jnp.exp(-x))


def _const_spec(shape):
    nd = len(shape)
    return pl.BlockSpec(shape, lambda *_: (0,) * nd)


def _in_proj_kernel(x_ref, nw_ref, w_ref, wkv_ref, q_ref, kt_ref, vt_ref, ktb_ref, vtb_ref, z_ref, xbc_ref, dt_ref):
    xn = _rms(x_ref[0], nw_ref[...]).astype(BF16)
    tm = xn.shape[0]
    c = 0
    q_ref[0] = (_dot(xn, w_ref[:, c:c + SB_WIDTH]) * (SB_HEAD_DIM ** -0.5)).astype(BF16)
    c += SB_WIDTH
    for r, (t_ref, tb_ref) in enumerate(((kt_ref, ktb_ref), (vt_ref, vtb_ref))):
        t = _dot_nt(wkv_ref[r * SB_WIDTH:(r + 1) * SB_WIDTH, :], xn)
        t_ref[0] = t.reshape(SB_HEADS, SB_HEAD_DIM, tm)
        tb_ref[0] = t.astype(BF16)
    z_ref[0] = _dot(xn, w_ref[:, c:c + SSM_WIDTH])
    c += SSM_WIDTH
    xbc_ref[0] = _dot(xn, w_ref[:, c:c + CONV_CH])
    c += CONV_CH
    dt_ref[0] = _dot(xn, w_ref[:, c:c + DT_PAD])


def _in_proj(x, norm_w, w_rest, w_kv_t, tm):
    b, l, _ = x.shape
    row = lambda n: pl.BlockSpec((1, tm, n), lambda i, j: (i, j, 0))
    outs = [
        ((l, SB_WIDTH), BF16, row(SB_WIDTH)),
        ((SB_HEADS, SB_HEAD_DIM, l), F32, pl.BlockSpec((1, SB_HEADS, SB_HEAD_DIM, tm), lambda i, j: (i, 0, 0, j))),
        ((SB_HEADS, SB_HEAD_DIM, l), F32, pl.BlockSpec((1, SB_HEADS, SB_HEAD_DIM, tm), lambda i, j: (i, 0, 0, j))),
        ((SB_WIDTH, l), BF16, pl.BlockSpec((1, SB_WIDTH, tm), lambda i, j: (i, 0, j))),
        ((SB_WIDTH, l), BF16, pl.BlockSpec((1, SB_WIDTH, tm), lambda i, j: (i, 0, j))),
        ((l, SSM_WIDTH), F32, row(SSM_WIDTH)), ((l, CONV_CH), F32, row(CONV_CH)), ((l, DT_PAD), F32, row(DT_PAD)),
    ]
    return pl.pallas_call(
        _in_proj_kernel,
        grid=(b, l // tm),
        in_specs=[row(D_MODEL), _const_spec((1, D_MODEL)), _const_spec(w_rest.shape), _const_spec(w_kv_t.shape)],
        out_specs=[spec for _, _, spec in outs],
        out_shape=[jax.ShapeDtypeStruct((b,) + shape, d) for shape, d, _ in outs],
        compiler_params=pltpu.CompilerParams(
            dimension_semantics=("parallel", "parallel"), vmem_limit_bytes=VMEM_LIMIT),
        name="in_proj",
    )(x, norm_w, w_rest, w_kv_t)


def _cum_matrix(tk):
    j = np.arange(tk)[:, None]
    s = np.arange(tk)[None, :]
    later = -(j > s).astype(np.float32)
    total = -np.ones((tk, LANES), np.float32)
    half = np.concatenate([later, total], axis=1)
    return jnp.asarray(np.concatenate([half, half], axis=0), dtype=BF16)


def _attn_kernel(*refs, tq, tk, tkd, n_past, nq, hps):
    if n_past:
        q_ref, kn_ref, vn_ref, kp_ref, vp_ref, ud_ref, uo_ref, o_ref = refs
    else:
        q_ref, kn_ref, vn_ref, ud_ref, uo_ref, o_ref = refs
    lane = lax.broadcasted_iota(jnp.int32, (1, LANES), 1)
    lo_lanes = lane < SB_HEAD_DIM
    lo_rows = lax.broadcasted_iota(jnp.int32, (LANES, 1), 0) < SB_HEAD_DIM
    n_fast = 2

    def blocks(groups):
        zs = [[[_dot(q_heads[h][r0:r1], kt_blk) for h in range(2)]
               for kt_blk, _, _, _, _, (r0, r1) in specs] for q_heads, specs, _ in groups]
        mids = []
        for (_, specs, _), zg in zip(groups, zs):
            mg = []
            for (_, _, u_ref, width, visible, _), zb in zip(specs, zg):
                mid = []
                for z in zb:
                    sp = jnp.maximum(z, 0.0) + jnp.log(1.0 + jnp.exp(-jnp.abs(z)))
                    spm = sp if visible is None else jnp.where(visible, sp, 0.0)
                    hi, lo = _split(spm, 2)
                    cs = _dot(jnp.concatenate([hi, lo], axis=1), u_ref[...])
                    mid.append((z - sp, cs))
                mg.append(mid)
            mids.append(mg)
        states = []
        for (_, specs, state), mg in zip(groups, mids):
            for (_, vt_blk, _, width, visible, (r0, r1)), mid in zip(specs, mg):
                new_carries = []
                probs = []
                for h, (log_beta, cs) in enumerate(mid):
                    carry = state[h][r0:r1]
                    p = jnp.exp(log_beta + cs[:, :width] + carry[:, :width])
                    if visible is not None:
                        p = jnp.where(visible, p, 0.0)
                    probs.append(p.astype(BF16))
                    new_carries.append(carry + cs[:, width:])
                vz = jnp.zeros_like(vt_blk)
                vcat = jnp.concatenate([jnp.where(lo_rows, vt_blk, vz), jnp.where(lo_rows, vz, vt_blk)], axis=1)
                acc = state[2][r0:r1] + _dot_nt(jnp.concatenate(probs, axis=1), vcat)
                out = (new_carries[0], new_carries[1], acc)
                if (r0, r1) != (0, tq):
                    out = tuple(jnp.concatenate(([old[:r0]] if r0 else []) + [new] + ([old[r1:]] if r1 < tq else []),
                                                axis=0) for old, new in zip(state, out))
                state = out
            states.append(state)
        return states

    def live(state, rows):
        return jnp.max(jnp.maximum(state[0][rows[0]:rows[1]], state[1][rows[0]:rows[1]])) > EXIT_LOG

    def kv_block(k_ref, v_ref, hp, start, size):
        if len(k_ref.shape) == 3:
            rows = slice(hp * LANES, (hp + 1) * LANES)
            return k_ref[0, rows, pl.ds(start, size)], v_ref[0, rows, pl.ds(start, size)]
        return tuple(r[0, 2 * hp:2 * hp + 2, :, pl.ds(start, size)].reshape(LANES, size).astype(BF16)
                     for r in (k_ref, v_ref))

    def sweep(q_heads, k_ref, v_ref, hp, n_blocks, skip, group, state):
        def cond(c):
            return jnp.logical_and(c[0] < (n_blocks - skip) // group, c[1])

        def body(c):
            i, _, st = c
            specs = []
            for g in range(group):
                start = pl.multiple_of((n_blocks - 1 - skip - (i * group + g)) * tk, tk)
                specs.append(kv_block(k_ref, v_ref, hp, start, tk) + (uo_ref, tk, None, (0, tq)))
            st = blocks([(q_heads, specs, st)])[0]
            return i + 1, live(st, (0, tq)), st

        return lax.while_loop(cond, body, (jnp.int32(0), live(state, (0, tq)), state))[2]

    def tile_head(hp, qi, with_new):
        q0 = qi * tq if isinstance(qi, int) else pl.multiple_of(qi * tq, tq)
        q2 = q_ref[0, pl.ds(q0, tq), hp * LANES:(hp + 1) * LANES]
        qz = jnp.zeros_like(q2)
        q_heads = (jnp.where(lo_lanes, q2, qz), jnp.where(lo_lanes, qz, q2))
        zeros = jnp.zeros((tq, LANES), F32)
        state = (zeros, zeros, zeros)
        specs = []
        for d in reversed(range(tq // tkd)):
            rows = tq - d * tkd
            visible = (lax.broadcasted_iota(jnp.int32, (rows, tkd), 1)
                       < lax.broadcasted_iota(jnp.int32, (rows, tkd), 0))
            specs.append(kv_block(kn_ref, vn_ref, hp, q0 + d * tkd, tkd) + (ud_ref, tkd, visible, (d * tkd, tq)))
        late = None
        if with_new:
            for j in range(n_fast):
                kv = kv_block(kn_ref, vn_ref, hp, pl.multiple_of(q0 - (j + 1) * tk, tk), tk)
                if j == 0 or tq == tk:
                    specs.append(kv + (uo_ref, tk, None, (0, tq)))
                else:
                    specs.append(kv + (uo_ref, tk, None, (0, tk)))
                    late = kv + (uo_ref, tk, None, (tk, tq))
        elif n_past:
            for j in range(n_fast):
                specs.append(kv_block(kp_ref, vp_ref, hp, (n_past - 1 - j) * tk, tk) + (uo_ref, tk, None, (0, tq)))
        return q0, q_heads, specs, late, state

    def tile_tail(hp, qi, q0, q_heads, state, with_new):
        if with_new:
            state = sweep(q_heads, kn_ref, vn_ref, hp, qi * (tq // tk), n_fast, tq // tk, state)
        if n_past:
            state = sweep(q_heads, kp_ref, vp_ref, hp, n_past, 0 if with_new else n_fast, 2, state)
        o_ref[0, pl.ds(q0, tq), hp * LANES:(hp + 1) * LANES] = state[2].astype(BF16)

    def tiles(qi, with_new):
        heads = [tile_head(hp, qi, with_new) for hp in range(hps)]
        states = blocks([(q_heads, specs, state) for _, q_heads, specs, _, state in heads])
        for hp, ((q0, q_heads, _, late, _), state) in enumerate(zip(heads, states)):
            if late is not None:
                state = lax.cond(live(state, (tk, tq)), lambda st: blocks([(q_heads, [late], st)])[0],
                                 lambda st: st, state)
            tile_tail(hp, qi, q0, q_heads, state, with_new)

    tiles(0, False)
    if nq > 1:
        def tile_body(qi, c):
            tiles(qi, True)
            return c

        lax.fori_loop(1, nq, tile_body, 0)


def _attention(q, kn, vn, kp, vp, tq, tk, hps):
    b, l, _ = q.shape
    tkd = min(tk, tq)
    nq = l // tq
    n_past = 0 if kp is None else kp.shape[3] // tk
    assert (nq == 1 or tq == 2 * tk) and (n_past == 0 or (nq == 1 and n_past % 2 == 0 and n_past >= 2))
    w = hps * LANES
    seq = lambda n: pl.BlockSpec((1, n, w), lambda bi, hi: (bi, 0, hi))
    seq_t = pl.BlockSpec((1, w, l), lambda bi, hi: (bi, hi, 0))
    ud, uo = _cum_matrix(tkd), _cum_matrix(tk)
    in_specs = [seq(l), seq_t, seq_t]
    args = [q, kn, vn]
    if n_past:
        assert w == SB_WIDTH
        past = pl.BlockSpec((1,) + kp.shape[1:], lambda bi, hi: (bi, 0, 0, 0))
        in_specs += [past, past]
        args += [kp, vp]
    in_specs += [_const_spec(ud.shape), _const_spec(uo.shape)]
    args += [ud, uo]
    return pl.pallas_call(
        functools.partial(_attn_kernel, tq=tq, tk=tk, tkd=tkd, n_past=n_past, nq=nq, hps=hps),
        grid=(b, SB_WIDTH // w),
        in_specs=in_specs,
        out_specs=seq(l),
        out_shape=jax.ShapeDtypeStruct((b, l, SB_WIDTH), BF16),
        compiler_params=pltpu.CompilerParams(
            dimension_semantics=("parallel", "parallel"), vmem_limit_bytes=VMEM_LIMIT),
        name="sb_attn",
    )(*args)


def _ssd_kernel(xbc_ref, z_ref, dt_ref, cw_ref, cb_ref, dtb_ref, alog_ref, dskip_ref, nw_ref,
                conv0_ref, st0_ref, ltri_ref, expand_ref, eye_ref, bmask_ref,
                y_ref, convout_ref, stout_ref, xp_scr, st_scr, *, q, cps):
    j = pl.program_id(1)
    rows = q * cps

    @pl.when(j == 0)
    def _():
        st_scr[...] = st0_ref[0]
        xp_scr[CONV_PAD - (CONV_WIDTH - 1):CONV_PAD, :] = conv0_ref[0]

    xp_scr[CONV_PAD:CONV_PAD + rows, :] = xbc_ref[0]
    base = CONV_PAD - (CONV_WIDTH - 1)
    xp = xp_scr[...]
    conv = xp * cw_ref[0:1, :]
    for i in range(1, CONV_WIDTH):
        conv = pltpu.roll(conv, 1, 0) + xp * cw_ref[i:i + 1, :]
    conv = conv[CONV_PAD:] + cb_ref[...]
    tail = xp_scr[base + rows:CONV_PAD + rows, :]
    xp_scr[base:CONV_PAD, :] = tail
    convout_ref[0] = tail
    xact = _silu(conv)
    xs_all = xact[:, :SSM_WIDTH]
    b2_all = xact[:, SSM_WIDTH:SSM_WIDTH + LANES].astype(BF16)
    c2_all = xact[:, SSM_WIDTH + LANES:].astype(BF16)
    dt_all = _softplus(dt_ref[0] + dtb_ref[...])
    a_all = dt_all * (-jnp.exp(alog_ref[...]))

    ltri = ltri_ref[...]
    expand = expand_ref[...]
    lane = lax.broadcasted_iota(jnp.int32, (1, LANES), 1)
    lo_lanes = lane < SSM_STATE
    causal = lax.broadcasted_iota(jnp.int32, (q, q), 0) >= lax.broadcasted_iota(jnp.int32, (q, q), 1)
    xz = jnp.zeros((q, LANES), BF16)
    st = st_scr[...]
    ys = []
    for c in range(cps):
        sl = slice(c * q, (c + 1) * q)
        xs, b2, c2b, dt = xs_all[sl], b2_all[sl], c2_all[sl], dt_all[sl]
        acum = _dot(ltri, jnp.concatenate(_split(a_all[sl], 3), axis=0))
        acum_p = jnp.concatenate(_split(acum, 3), axis=1)
        acum_t = _dot_nt(eye_ref[...], acum_p)
        acum_e = _dot(acum_p, expand)
        dt_e = _dot(jnp.concatenate(_split(dt, 3), axis=1), expand)
        xdt = xs * dt_e
        xdt_b = xdt.astype(BF16)
        end_e = acum_e[q - 1:q, :]
        xw = (xdt * jnp.exp(end_e - acum_e)).astype(BF16)
        y = _dot(c2b, st.astype(BF16)) * jnp.exp(acum_e)
        st = st * jnp.exp(end_e) + bmask_ref[...] * _dot_tn(b2, xw)
        cz = jnp.zeros_like(c2b)
        y_pairs = []
        for g in range(SSM_GROUPS):
            cg = jnp.where(lo_lanes, c2b, cz) if g == 0 else jnp.where(lo_lanes, cz, c2b)
            gram = _dot_nt(cg, b2)
            for pair in range(2):
                ms = []
                for hh in range(2):
                    h = 4 * g + 2 * pair + hh
                    d = acum[:, h:h + 1] - acum_t[h:h + 1, :]
                    decay = jnp.exp(jnp.where(causal, d, -1e30))
                    ms.append((gram * decay).astype(BF16))
                xd = xdt_b[:, (2 * g + pair) * LANES:(2 * g + pair + 1) * LANES]
                xcat = jnp.concatenate([jnp.where(lo_lanes, xd, xz), jnp.where(lo_lanes, xz, xd)], axis=0)
                y_pairs.append(_dot(jnp.concatenate(ms, axis=1), xcat))
        ys.append(y + jnp.concatenate(y_pairs, axis=1))
    st_scr[...] = st
    y = (ys[0] if cps == 1 else jnp.concatenate(ys, axis=0)) + dskip_ref[...] * xs_all
    y = y * _silu(z_ref[0])
    gw = SSM_WIDTH // SSM_GROUPS
    outs = []
    for g in range(SSM_GROUPS):
        yg = y[:, g * gw:(g + 1) * gw]
        outs.append(yg * lax.rsqrt(jnp.mean(yg * yg, axis=-1, keepdims=True) + EPS))
    y_ref[0] = (jnp.concatenate(outs, axis=1) * nw_ref[...]).astype(BF16)

    @pl.when(j == pl.num_programs(1) - 1)
    def _():
        stout_ref[0] = st


def _ssd(xbc, z, dt, conv_w, conv_b, dt_bias, a_log, d_skip, ssm_norm_w, conv0, st0, q, cps):
    b, l, _ = xbc.shape
    rows = q * cps
    tri = jnp.asarray(np.tile(np.tril(np.ones((q, q), np.float32)), (1, 3)), dtype=BF16)
    expand = np.zeros((LANES, SSM_WIDTH), np.float32)
    for h in range(SSM_HEADS):
        expand[h, h * SSM_HEAD_DIM:(h + 1) * SSM_HEAD_DIM] = 1.0
    expand = np.tile(expand, (3, 1))
    eye = jnp.asarray(np.tile(np.eye(LANES, dtype=np.float32), (1, 3)), dtype=BF16)
    pad = lambda v: jnp.pad(v.astype(F32), (0, LANES - v.shape[0])).reshape(1, LANES)
    seq = lambda n: pl.BlockSpec((1, rows, n), lambda bi, ji: (bi, ji, 0))
    per_b = lambda s: pl.BlockSpec((1,) + s, lambda bi, ji: (bi, 0, 0))
    consts = [
        conv_w.astype(F32), conv_b.astype(F32).reshape(1, CONV_CH), pad(dt_bias), pad(a_log),
        jnp.repeat(d_skip.astype(F32), SSM_HEAD_DIM).reshape(1, SSM_WIDTH),
        ssm_norm_w.astype(F32).reshape(1, SSM_WIDTH),
    ]
    mats = [tri, jnp.asarray(expand, dtype=BF16), eye, _state_block_mask()]
    return pl.pallas_call(
        functools.partial(_ssd_kernel, q=q, cps=cps),
        grid=(b, l // rows),
        in_specs=[seq(CONV_CH), seq(SSM_WIDTH), seq(DT_PAD)]
        + [_const_spec(c.shape) for c in consts]
        + [per_b((CONV_WIDTH - 1, CONV_CH)), per_b((LANES, SSM_WIDTH))]
        + [_const_spec(m.shape) for m in mats],
        out_specs=[seq(SSM_WIDTH), per_b((CONV_WIDTH - 1, CONV_CH)), per_b((LANES, SSM_WIDTH))],
        out_shape=[
            jax.ShapeDtypeStruct((b, l, SSM_WIDTH), BF16),
            jax.ShapeDtypeStruct((b, CONV_WIDTH - 1, CONV_CH), F32),
            jax.ShapeDtypeStruct((b, LANES, SSM_WIDTH), F32),
        ],
        scratch_shapes=[
            pltpu.VMEM((CONV_PAD + rows, CONV_CH), F32),
            pltpu.VMEM((LANES, SSM_WIDTH), F32),
        ],
        compiler_params=pltpu.CompilerParams(
            dimension_semantics=("parallel", "arbitrary"), vmem_limit_bytes=VMEM_LIMIT),
        name="conv_ssd",
    )(xbc, z, dt, *consts, conv0, st0, *mats)


def _state_block_mask():
    m = np.zeros((LANES, SSM_WIDTH), np.float32)
    per = SSM_WIDTH // SSM_GROUPS
    for g in range(SSM_GROUPS):
        m[g * SSM_STATE:(g + 1) * SSM_STATE, g * per:(g + 1) * per] = 1.0
    return jnp.asarray(m)


def _state_to_packed(h):
    b = h.shape[0]
    hg = SSM_HEADS // SSM_GROUPS
    x = h.astype(F32).reshape(b, SSM_GROUPS, hg, SSM_HEAD_DIM, SSM_STATE)
    x = x.transpose(0, 1, 4, 2, 3).reshape(b, SSM_GROUPS, SSM_STATE, hg * SSM_HEAD_DIM)
    z = jnp.zeros_like(x)
    rows = [jnp.concatenate([x[:, g] if g2 == g else z[:, g] for g2 in range(SSM_GROUPS)], axis=-1)
            for g in range(SSM_GROUPS)]
    return jnp.concatenate(rows, axis=1)


def _state_from_packed(st):
    b = st.shape[0]
    hg = SSM_HEADS // SSM_GROUPS
    per = hg * SSM_HEAD_DIM
    blocks = [st[:, g * SSM_STATE:(g + 1) * SSM_STATE, g * per:(g + 1) * per] for g in range(SSM_GROUPS)]
    x = jnp.stack(blocks, axis=1).reshape(b, SSM_GROUPS, SSM_STATE, hg, SSM_HEAD_DIM)
    return x.transpose(0, 1, 3, 4, 2).reshape(b, SSM_HEADS, SSM_HEAD_DIM, SSM_STATE)


def _tail_kernel(x_ref, o_ref, ys_ref, woa_ref, wob_ref, n2_ref, wup_ref, wdn_ref, fn_ref, y_ref, *, ff_chunk):
    h = x_ref[...] + _dot(o_ref[...], woa_ref[...]) + _dot(ys_ref[...], wob_ref[...])
    hn = _rms(h, n2_ref[...]).astype(BF16)
    ffn = None
    for c in range(0, D_FF, ff_chunk):
        u = jnp.maximum(_dot(hn, wup_ref[:, c:c + ff_chunk]), 0.0)
        d = _dot((u * u).astype(BF16), wdn_ref[c:c + ff_chunk, :])
        ffn = d if ffn is None else ffn + d
    y_ref[...] = _rms(h + ffn, fn_ref[...])


def _tail(x2d, o2d, ys2d, wo_a, wo_b, n2, wup, wdn, fnw, tm, ff_chunk=1024):
    t = x2d.shape[0]
    row = lambda n: pl.BlockSpec((tm, n), lambda i: (i, 0))
    single = lambda s: pl.BlockSpec(s, lambda i: (0, 0), pipeline_mode=pl.Buffered(1))
    return pl.pallas_call(
        functools.partial(_tail_kernel, ff_chunk=ff_chunk),
        grid=(t // tm,),
        in_specs=[row(D_MODEL), row(SB_WIDTH), row(SSM_WIDTH),
                  single(wo_a.shape), single(wo_b.shape), single((1, D_MODEL)),
                  single(wup.shape), single(wdn.shape), single((1, D_MODEL))],
        out_specs=row(D_MODEL),
        out_shape=jax.ShapeDtypeStruct((t, D_MODEL), F32),
        compiler_params=pltpu.CompilerParams(
            dimension_semantics=("parallel",), vmem_limit_bytes=VMEM_LIMIT),
        name="out_ffn",
    )(x2d, o2d, ys2d, wo_a, wo_b, n2, wup, wdn, fnw)


def _pick(n, prefs):
    for p in prefs:
        if n % p == 0:
            return p
    raise ValueError(f"no tile in {prefs} divides {n}")


def _layer(x, k_past, v_past, conv0, st0, w):
    b, l, _ = x.shape
    t = b * l
    tm = _pick(t, (512, 256, 128, 64))
    if l % tm == 0:
        q, kt, vt, ktb, vtb, z, xbc, dt = _in_proj(x, w["norm1"], w["w_rest"], w["w_kv_t"], tm)
    else:
        outs = _in_proj(x.reshape(1, t, D_MODEL), w["norm1"], w["w_rest"], w["w_kv_t"], tm)
        split_t = lambda a: jnp.moveaxis(a.reshape(a.shape[1:-1] + (b, l)), -2, 0)
        q, z, xbc, dt = (a.reshape(b, l, a.shape[-1]) for a in (outs[0], outs[5], outs[6], outs[7]))
        kt, vt, ktb, vtb = (split_t(a) for a in outs[1:5])
    tq = _pick(l, (256, 128, 64))
    o = _attention(q, ktb, vtb, k_past, v_past, tq, LANES, 2 if k_past is None else SB_WIDTH // LANES)
    q_ssd = _pick(l, (128, 64))
    ys, conv_new, st_new = _ssd(xbc, z, dt, w["conv_w"], w["conv_b"], w["dt_bias"], w["a_log"],
                                w["d_skip"], w["ssm_norm_w"], conv0, st0, q_ssd, 4 if l % (4 * q_ssd) == 0 else 1)
    y = _tail(x.reshape(t, D_MODEL), o.reshape(t, SB_WIDTH), ys.reshape(t, SSM_WIDTH),
              w["wo_a"], w["wo_b"], w["norm2"], w["w_up"], w["w_down"], w["final"], tm)
    return y.reshape(b, l, D_MODEL), kt, vt, conv_new, _state_from_packed(st_new)


def kernel(x_prompt, x_sample, cache_k, cache_v, state_conv, state_ssm, norm1_w, w_in, conv_w, conv_b,
           dt_bias, a_log, d_skip, ssm_norm_w, w_out, norm2_w, w_up, w_down, final_norm_w):
    assert w_in.shape[0] == 1, "one layer: the final norm is fused into the layer's last kernel"
    bp = x_prompt.shape[0]
    in_cols = w_in.shape[-1]
    w = {
        "norm1": norm1_w[0].reshape(1, D_MODEL),
        "w_rest": jnp.pad(jnp.concatenate([w_in[0][:, :SB_WIDTH], w_in[0][:, 3 * SB_WIDTH:]], axis=1),
                          ((0, 0), (0, DT_PAD - SSM_HEADS))).astype(BF16),
        "w_kv_t": w_in[0][:, SB_WIDTH:3 * SB_WIDTH].T.astype(BF16),
        "conv_w": conv_w[0], "conv_b": conv_b[0], "dt_bias": dt_bias[0], "a_log": a_log[0],
        "d_skip": d_skip[0], "ssm_norm_w": ssm_norm_w[0],
        "wo_a": w_out[0, :SB_WIDTH].astype(BF16), "wo_b": w_out[0, SB_WIDTH:].astype(BF16),
        "norm2": norm2_w[0].reshape(1, D_MODEL),
        "w_up": w_up[0].astype(BF16), "w_down": w_down[0].astype(BF16),
        "final": final_norm_w.reshape(1, D_MODEL),
    }
    assert in_cols == 3 * SB_WIDTH + SSM_WIDTH + CONV_CH + SSM_HEADS
    conv_zero = jnp.zeros((bp, CONV_WIDTH - 1, CONV_CH), F32)
    st_zero = jnp.zeros((bp, LANES, SSM_WIDTH), F32)
    yp, kp, vp, cp, sp = _layer(x_prompt, None, None, conv_zero, st_zero, w)
    to_t = lambda a: jnp.transpose(a, (0, 2, 3, 1))
    from_t = lambda a: jnp.transpose(a, (0, 3, 1, 2))[None]
    ys, ks, vs, cs, ss = _layer(x_sample, to_t(cache_k[0]), to_t(cache_v[0]), state_conv[0],
                                _state_to_packed(state_ssm[0]), w)
    return (yp, ys, from_t(kp), from_t(vp), cp[None], sp[None], from_t(ks), from_t(vs), cs[None], ss[None])
```

```python
import functools

import numpy as np
import jax
import jax.numpy as jnp
from jax import lax
from jax.experimental import pallas as pl
from jax.experimental.pallas import tpu as pltpu

F32 = jnp.float32
BF16 = jnp.bfloat16

D_MODEL = 1024
SB_HEADS = 8
SB_HEAD_DIM = 64
SB_WIDTH = SB_HEADS * SB_HEAD_DIM
SSM_HEADS = 8
SSM_HEAD_DIM = 64
SSM_WIDTH = SSM_HEADS * SSM_HEAD_DIM
SSM_GROUPS = 2
SSM_STATE = 64
CONV_WIDTH = 4
CONV_CH = SSM_WIDTH + 2 * SSM_GROUPS * SSM_STATE
D_FF = 4 * D_MODEL
EPS = 1e-5

LANES = 128
CONV_PAD = 8
DT_PAD = LANES
VMEM_LIMIT = 56 * 1024 * 1024
EXIT_LOG = -105.0


def _dot(a, b):
    return jnp.dot(a, b, preferred_element_type=F32)


def _dot_nt(a, b):
    return lax.dot_general(a, b, (((1,), (1,)), ((), ())), preferred_element_type=F32)


def _dot_tn(a, b):
    return lax.dot_general(a, b, (((0,), (0,)), ((), ())), preferred_element_type=F32)


def _split(x, pieces):
    out = []
    r = x
    for i in range(pieces):
        p = r.astype(BF16)
        out.append(p)
        if i + 1 < pieces:
            r = r - p.astype(F32)
    return out


def _rms(x, w):
    return x * lax.rsqrt(jnp.mean(x * x, axis=-1, keepdims=True) + EPS) * w


def _softplus(x):
    return jnp.maximum(x, 0.0) + jnp.log1p(jnp.exp(-jnp.abs(x)))


def _silu(x):
    return x / (1.0 + jnp.exp(-x))


def _const_spec(shape):
    nd = len(shape)
    return pl.BlockSpec(shape, lambda *_: (0,) * nd)


def _in_proj_kernel(x_ref, nw_ref, w_ref, wkv_ref, q_ref, kt_ref, vt_ref, ktb_ref, vtb_ref, z_ref, xbc_ref, dt_ref):
    xn = _rms(x_ref[0], nw_ref[...]).astype(BF16)
    tm = xn.shape[0]
    c = 0
    q_ref[0] = (_dot(xn, w_ref[:, c:c + SB_WIDTH]) * (SB_HEAD_DIM ** -0.5)).astype(BF16)
    c += SB_WIDTH
    for r, (t_ref, tb_ref) in enumerate(((kt_ref, ktb_ref), (vt_ref, vtb_ref))):
        t = _dot_nt(wkv_ref[r * SB_WIDTH:(r + 1) * SB_WIDTH, :], xn)
        t_ref[0] = t.reshape(SB_HEADS, SB_HEAD_DIM, tm)
        tb_ref[0] = t.astype(BF16)
    z_ref[0] = _dot(xn, w_ref[:, c:c + SSM_WIDTH])
    c += SSM_WIDTH
    xbc_ref[0] = _dot(xn, w_ref[:, c:c + CONV_CH])
    c += CONV_CH
    dt_ref[0] = _dot(xn, w_ref[:, c:c + DT_PAD])


def _in_proj(x, norm_w, w_rest, w_kv_t, tm):
    b, l, _ = x.shape
    row = lambda n: pl.BlockSpec((1, tm, n), lambda i, j: (i, j, 0))
    outs = [
        ((l, SB_WIDTH), BF16, row(SB_WIDTH)),
        ((SB_HEADS, SB_HEAD_DIM, l), F32, pl.BlockSpec((1, SB_HEADS, SB_HEAD_DIM, tm), lambda i, j: (i, 0, 0, j))),
        ((SB_HEADS, SB_HEAD_DIM, l), F32, pl.BlockSpec((1, SB_HEADS, SB_HEAD_DIM, tm), lambda i, j: (i, 0, 0, j))),
        ((SB_WIDTH, l), BF16, pl.BlockSpec((1, SB_WIDTH, tm), lambda i, j: (i, 0, j))),
        ((SB_WIDTH, l), BF16, pl.BlockSpec((1, SB_WIDTH, tm), lambda i, j: (i, 0, j))),
        ((l, SSM_WIDTH), F32, row(SSM_WIDTH)), ((l, CONV_CH), F32, row(CONV_CH)), ((l, DT_PAD), F32, row(DT_PAD)),
    ]
    return pl.pallas_call(
        _in_proj_kernel,
        grid=(b, l // tm),
        in_specs=[row(D_MODEL), _const_spec((1, D_MODEL)), _const_spec(w_rest.shape), _const_spec(w_kv_t.shape)],
        out_specs=[spec for _, _, spec in outs],
        out_shape=[jax.ShapeDtypeStruct((b,) + shape, d) for shape, d, _ in outs],
        compiler_params=pltpu.CompilerParams(
            dimension_semantics=("parallel", "parallel"), vmem_limit_bytes=VMEM_LIMIT),
        name="in_proj",
    )(x, norm_w, w_rest, w_kv_t)


def _cum_matrix(tk):
    j = np.arange(tk)[:, None]
    s = np.arange(tk)[None, :]
    later = -(j > s).astype(np.float32)
    total = -np.ones((tk, LANES), np.float32)
    half = np.concatenate([later, total], axis=1)
    return jnp.asarray(np.concatenate([half, half], axis=0), dtype=BF16)


def _attn_kernel(*refs, tq, tk, tkd, n_past, nq, hps):
    if n_past:
        q_ref, kn_ref, vn_ref, kp_ref, vp_ref, ud_ref, uo_ref, o_ref = refs
    else:
        q_ref, kn_ref, vn_ref, ud_ref, uo_ref, o_ref = refs
    lane = lax.broadcasted_iota(jnp.int32, (1, LANES), 1)
    lo_lanes = lane < SB_HEAD_DIM
    lo_rows = lax.broadcasted_iota(jnp.int32, (LANES, 1), 0) < SB_HEAD_DIM
    n_fast = 2

    def logits(q_heads, specs, pairs):
        out = [None] * len(specs)
        for i, j in pairs:
            (kt_i, _, _, _, _, (i0, i1)), (kt_j, _, _, width, _, (j0, j1)) = specs[i], specs[j]
            u0, u1 = min(i0, j0), max(i1, j1)
            both = [_dot(q_heads[h][u0:u1], jnp.concatenate([kt_j, kt_i], axis=1)) for h in range(2)]
            out[j] = [z[j0 - u0:j1 - u0, :width] for z in both]
            out[i] = [z[i0 - u0:i1 - u0, width:] for z in both]
        for n, (kt_blk, _, _, _, _, (r0, r1)) in enumerate(specs):
            if out[n] is None:
                out[n] = [_dot(q_heads[h][r0:r1], kt_blk) for h in range(2)]
        return out

    def blocks(groups):
        zs = [logits(q_heads, specs, pairs) for q_heads, specs, _, pairs in groups]
        mids = []
        for (_, specs, _, _), zg in zip(groups, zs):
            mg = []
            for (_, _, u_ref, width, visible, _), zb in zip(specs, zg):
                mid = []
                for z in zb:
                    sp = jnp.maximum(z, 0.0) + jnp.log(1.0 + jnp.exp(-jnp.abs(z)))
                    spm = sp if visible is None else jnp.where(visible, sp, 0.0)
                    hi, lo = _split(spm, 2)
                    cs = _dot(jnp.concatenate([hi, lo], axis=1), u_ref[...])
                    mid.append((z - sp, cs))
                mg.append(mid)
            mids.append(mg)
        states = []
        for (_, specs, state, _), mg in zip(groups, mids):
            for (_, vt_blk, _, width, visible, (r0, r1)), mid in zip(specs, mg):
                new_carries = []
                probs = []
                for h, (log_beta, cs) in enumerate(mid):
                    carry = state[h][r0:r1]
                    p = jnp.exp(log_beta + cs[:, :width] + carry[:, :width])
                    if visible is not None:
                        p = jnp.where(visible, p, 0.0)
                    probs.append(p.astype(BF16))
                    new_carries.append(carry + cs[:, width:])
                vz = jnp.zeros_like(vt_blk)
                vcat = jnp.concatenate([jnp.where(lo_rows, vt_blk, vz), jnp.where(lo_rows, vz, vt_blk)], axis=1)
                acc = state[2][r0:r1] + _dot_nt(jnp.concatenate(probs, axis=1), vcat)
                out = (new_carries[0], new_carries[1], acc)
                if (r0, r1) != (0, tq):
                    out = tuple(jnp.concatenate(([old[:r0]] if r0 else []) + [new] + ([old[r1:]] if r1 < tq else []),
                                                axis=0) for old, new in zip(state, out))
                state = out
            states.append(state)
        return states

    def live(state, rows):
        return jnp.max(jnp.maximum(state[0][rows[0]:rows[1]], state[1][rows[0]:rows[1]])) > EXIT_LOG

    def kv_block(k_ref, v_ref, hp, start, size):
        if len(k_ref.shape) == 3:
            rows = slice(hp * LANES, (hp + 1) * LANES)
            return k_ref[0, rows, pl.ds(start, size)], v_ref[0, rows, pl.ds(start, size)]
        return tuple(r[0, 2 * hp:2 * hp + 2, :, pl.ds(start, size)].reshape(LANES, size).astype(BF16)
                     for r in (k_ref, v_ref))

    def sweep(q_heads, k_ref, v_ref, hp, n_blocks, skip, group, state):
        def cond(c):
            return jnp.logical_and(c[0] < (n_blocks - skip) // group, c[1])

        def body(c):
            i, _, st = c
            specs = []
            for g in range(group):
                start = pl.multiple_of((n_blocks - 1 - skip - (i * group + g)) * tk, tk)
                specs.append(kv_block(k_ref, v_ref, hp, start, tk) + (uo_ref, tk, None, (0, tq)))
            st = blocks([(q_heads, specs, st, [(0, 1)] if group == 2 else [])])[0]
            return i + 1, live(st, (0, tq)), st

        return lax.while_loop(cond, body, (jnp.int32(0), live(state, (0, tq)), state))[2]

    def tile_head(hp, qi, with_new):
        q0 = qi * tq if isinstance(qi, int) else pl.multiple_of(qi * tq, tq)
        q2 = q_ref[0, pl.ds(q0, tq), hp * LANES:(hp + 1) * LANES]
        qz = jnp.zeros_like(q2)
        q_heads = (jnp.where(lo_lanes, q2, qz), jnp.where(lo_lanes, qz, q2))
        zeros = jnp.zeros((tq, LANES), F32)
        state = (zeros, zeros, zeros)
        specs = []
        for d in reversed(range(tq // tkd)):
            rows = tq - d * tkd
            visible = (lax.broadcasted_iota(jnp.int32, (rows, tkd), 1)
                       < lax.broadcasted_iota(jnp.int32, (rows, tkd), 0))
            specs.append(kv_block(kn_ref, vn_ref, hp, q0 + d * tkd, tkd) + (ud_ref, tkd, visible, (d * tkd, tq)))
        n_diag = len(specs)
        pairs = [(0, 1)] if n_diag == 2 else []
        late = []
        if with_new:
            for j in range(n_fast):
                kv = kv_block(kn_ref, vn_ref, hp, pl.multiple_of(q0 - (j + 1) * tk, tk), tk)
                rows_j = max(tq - tk * j - tk // 2, tk // 2)
                specs.append(kv + (uo_ref, tk, None, (0, rows_j)))
                if rows_j < tq:
                    late.append(kv + (uo_ref, tk, None, (rows_j, tq)))
            pairs.append((n_diag, n_diag + 1))
        elif n_past:
            for j in range(n_fast):
                specs.append(kv_block(kp_ref, vp_ref, hp, (n_past - 1 - j) * tk, tk) + (uo_ref, tk, None, (0, tq)))
            pairs.append((n_diag, n_diag + 1))
        return q0, q_heads, specs, pairs, late, state

    def tile_rest(hp, qi, q_heads, late, state, with_new):
        for spec in late:
            state = lax.cond(live(state, spec[5]), lambda st, spec=spec: blocks([(q_heads, [spec], st, [])])[0],
                             lambda st: st, state)
        if with_new:
            state = sweep(q_heads, kn_ref, vn_ref, hp, qi * (tq // tk), n_fast, tq // tk, state)
        if n_past:
            state = sweep(q_heads, kp_ref, vp_ref, hp, n_past, 0 if with_new else n_fast, 2, state)
        return state

    def tiles(qi, with_new):
        heads = [tile_head(hp, qi, with_new) for hp in range(hps)]
        states = blocks([(q_heads, specs, state, pairs) for _, q_heads, specs, pairs, _, state in heads])
        if with_new or n_past:
            def rest(sts):
                return [tile_rest(hp, qi, q_heads, late, st, with_new)
                        for hp, ((_, q_heads, _, _, late, _), st) in enumerate(zip(heads, sts))]

            top = functools.reduce(jnp.maximum, [c for st in states for c in st[:2]])
            states = lax.cond(jnp.max(top) > EXIT_LOG, rest, lambda sts: sts, states)
        for hp, ((q0, _, _, _, _, _), state) in enumerate(zip(heads, states)):
            o_ref[0, pl.ds(q0, tq), hp * LANES:(hp + 1) * LANES] = state[2].astype(BF16)

    tiles(0, False)
    if nq > 1:
        def tile_body(qi, c):
            tiles(qi, True)
            return c

        lax.fori_loop(1, nq, tile_body, 0)


def _attention(q, kn, vn, kp, vp, tq, tk, hps):
    b, l, _ = q.shape
    tkd = min(tk, tq)
    nq = l // tq
    n_past = 0 if kp is None else kp.shape[3] // tk
    assert (nq == 1 or tq == 2 * tk) and (n_past == 0 or (nq == 1 and n_past % 2 == 0 and n_past >= 2))
    w = hps * LANES
    seq = lambda n: pl.BlockSpec((1, n, w), lambda bi, hi: (bi, 0, hi))
    seq_t = pl.BlockSpec((1, w, l), lambda bi, hi: (bi, hi, 0))
    ud, uo = _cum_matrix(tkd), _cum_matrix(tk)
    in_specs = [seq(l), seq_t, seq_t]
    args = [q, kn, vn]
    if n_past:
        assert w == SB_WIDTH
        past = pl.BlockSpec((1,) + kp.shape[1:], lambda bi, hi: (bi, 0, 0, 0))
        in_specs += [past, past]
        args += [kp, vp]
    in_specs += [_const_spec(ud.shape), _const_spec(uo.shape)]
    args += [ud, uo]
    return pl.pallas_call(
        functools.partial(_attn_kernel, tq=tq, tk=tk, tkd=tkd, n_past=n_past, nq=nq, hps=hps),
        grid=(b, SB_WIDTH // w),
        in_specs=in_specs,
        out_specs=seq(l),
        out_shape=jax.ShapeDtypeStruct((b, l, SB_WIDTH), BF16),
        compiler_params=pltpu.CompilerParams(
            dimension_semantics=("parallel", "parallel"), vmem_limit_bytes=VMEM_LIMIT),
        name="sb_attn",
    )(*args)


def _ssd_kernel(xbc_ref, z_ref, dt_ref, cw_ref, cb_ref, dtb_ref, alog_ref, dskip_ref, nw_ref,
                conv0_ref, st0_ref, ltri_ref, expand_ref, eye_ref, bmask_ref,
                y_ref, convout_ref, stout_ref, xp_scr, st_scr, *, q, cps):
    j = pl.program_id(1)
    rows = q * cps

    @pl.when(j == 0)
    def _():
        st_scr[...] = st0_ref[0]
        xp_scr[CONV_PAD - (CONV_WIDTH - 1):CONV_PAD, :] = conv0_ref[0]

    xp_scr[CONV_PAD:CONV_PAD + rows, :] = xbc_ref[0]
    base = CONV_PAD - (CONV_WIDTH - 1)
    xp = xp_scr[...]
    conv = xp * cw_ref[0:1, :]
    for i in range(1, CONV_WIDTH):
        conv = pltpu.roll(conv, 1, 0) + xp * cw_ref[i:i + 1, :]
    conv = conv[CONV_PAD:] + cb_ref[...]
    tail = xp_scr[base + rows:CONV_PAD + rows, :]
    xp_scr[base:CONV_PAD, :] = tail
    convout_ref[0] = tail
    xact = _silu(conv)
    xs_all = xact[:, :SSM_WIDTH]
    b2_all = xact[:, SSM_WIDTH:SSM_WIDTH + LANES].astype(BF16)
    c2_all = xact[:, SSM_WIDTH + LANES:].astype(BF16)
    dt_all = _softplus(dt_ref[0] + dtb_ref[...])
    a_all = dt_all * (-jnp.exp(alog_ref[...]))

    ltri = ltri_ref[...]
    expand = expand_ref[...]
    lane = lax.broadcasted_iota(jnp.int32, (1, LANES), 1)
    lo_lanes = lane < SSM_STATE
    causal = lax.broadcasted_iota(jnp.int32, (q, q), 0) >= lax.broadcasted_iota(jnp.int32, (q, q), 1)
    xz = jnp.zeros((q, LANES), BF16)
    st = st_scr[...]
    ys = []
    for c in range(cps):
        sl = slice(c * q, (c + 1) * q)
        xs, b2, c2b, dt = xs_all[sl], b2_all[sl], c2_all[sl], dt_all[sl]
        acum = _dot(ltri, jnp.concatenate(_split(a_all[sl], 3), axis=0))
        acum_p = jnp.concatenate(_split(acum, 3), axis=1)
        acum_t = _dot_nt(eye_ref[...], acum_p)
        acum_e = _dot(acum_p, expand)
        dt_e = _dot(jnp.concatenate(_split(dt, 3), axis=1), expand)
        xdt = xs * dt_e
        xdt_b = xdt.astype(BF16)
        end_e = acum_e[q - 1:q, :]
        xw = (xdt * jnp.exp(end_e - acum_e)).astype(BF16)
        y = _dot(c2b, st.astype(BF16)) * jnp.exp(acum_e)
        st = st * jnp.exp(end_e) + bmask_ref[...] * _dot_tn(b2, xw)
        cz = jnp.zeros_like(c2b)
        y_pairs = []
        for g in range(SSM_GROUPS):
            cg = jnp.where(lo_lanes, c2b, cz) if g == 0 else jnp.where(lo_lanes, cz, c2b)
            gram = _dot_nt(cg, b2)
            for pair in range(2):
                ms = []
                for hh in range(2):
                    h = 4 * g + 2 * pair + hh
                    d = acum[:, h:h + 1] - acum_t[h:h + 1, :]
                    decay = jnp.exp(jnp.where(causal, d, -1e30))
                    ms.append((gram * decay).astype(BF16))
                xd = xdt_b[:, (2 * g + pair) * LANES:(2 * g + pair + 1) * LANES]
                xcat = jnp.concatenate([jnp.where(lo_lanes, xd, xz), jnp.where(lo_lanes, xz, xd)], axis=0)
                y_pairs.append(_dot(jnp.concatenate(ms, axis=1), xcat))
        ys.append(y + jnp.concatenate(y_pairs, axis=1))
    st_scr[...] = st
    y = (ys[0] if cps == 1 else jnp.concatenate(ys, axis=0)) + dskip_ref[...] * xs_all
    y = y * _silu(z_ref[0])
    gw = SSM_WIDTH // SSM_GROUPS
    outs = []
    for g in range(SSM_GROUPS):
        yg = y[:, g * gw:(g + 1) * gw]
        outs.append(yg * lax.rsqrt(jnp.mean(yg * yg, axis=-1, keepdims=True) + EPS))
    y_ref[0] = (jnp.concatenate(outs, axis=1) * nw_ref[...]).astype(BF16)

    @pl.when(j == pl.num_programs(1) - 1)
    def _():
        stout_ref[0] = st


def _ssd(xbc, z, dt, conv_w, conv_b, dt_bias, a_log, d_skip, ssm_norm_w, conv0, st0, q, cps):
    b, l, _ = xbc.shape
    rows = q * cps
    tri = jnp.asarray(np.tile(np.tril(np.ones((q, q), np.float32)), (1, 3)), dtype=BF16)
    expand = np.zeros((LANES, SSM_WIDTH), np.float32)
    for h in range(SSM_HEADS):
        expand[h, h * SSM_HEAD_DIM:(h + 1) * SSM_HEAD_DIM] = 1.0
    expand = np.tile(expand, (3, 1))
    eye = jnp.asarray(np.tile(np.eye(LANES, dtype=np.float32), (1, 3)), dtype=BF16)
    pad = lambda v: jnp.pad(v.astype(F32), (0, LANES - v.shape[0])).reshape(1, LANES)
    seq = lambda n: pl.BlockSpec((1, rows, n), lambda bi, ji: (bi, ji, 0))
    per_b = lambda s: pl.BlockSpec((1,) + s, lambda bi, ji: (bi, 0, 0))
    consts = [
        conv_w.astype(F32), conv_b.astype(F32).reshape(1, CONV_CH), pad(dt_bias), pad(a_log),
        jnp.repeat(d_skip.astype(F32), SSM_HEAD_DIM).reshape(1, SSM_WIDTH),
        ssm_norm_w.astype(F32).reshape(1, SSM_WIDTH),
    ]
    mats = [tri, jnp.asarray(expand, dtype=BF16), eye, _state_block_mask()]
    return pl.pallas_call(
        functools.partial(_ssd_kernel, q=q, cps=cps),
        grid=(b, l // rows),
        in_specs=[seq(CONV_CH), seq(SSM_WIDTH), seq(DT_PAD)]
        + [_const_spec(c.shape) for c in consts]
        + [per_b((CONV_WIDTH - 1, CONV_CH)), per_b((LANES, SSM_WIDTH))]
        + [_const_spec(m.shape) for m in mats],
        out_specs=[seq(SSM_WIDTH), per_b((CONV_WIDTH - 1, CONV_CH)), per_b((LANES, SSM_WIDTH))],
        out_shape=[
            jax.ShapeDtypeStruct((b, l, SSM_WIDTH), BF16),
            jax.ShapeDtypeStruct((b, CONV_WIDTH - 1, CONV_CH), F32),
            jax.ShapeDtypeStruct((b, LANES, SSM_WIDTH), F32),
        ],
        scratch_shapes=[
            pltpu.VMEM((CONV_PAD + rows, CONV_CH), F32),
            pltpu.VMEM((LANES, SSM_WIDTH), F32),
        ],
        compiler_params=pltpu.CompilerParams(
            dimension_semantics=("parallel", "arbitrary"), vmem_limit_bytes=VMEM_LIMIT),
        name="conv_ssd",
    )(xbc, z, dt, *consts, conv0, st0, *mats)


def _state_block_mask():
    m = np.zeros((LANES, SSM_WIDTH), np.float32)
    per = SSM_WIDTH // SSM_GROUPS
    for g in range(SSM_GROUPS):
        m[g * SSM_STATE:(g + 1) * SSM_STATE, g * per:(g + 1) * per] = 1.0
    return jnp.asarray(m)


def _state_to_packed(h):
    b = h.shape[0]
    hg = SSM_HEADS // SSM_GROUPS
    x = h.astype(F32).reshape(b, SSM_GROUPS, hg, SSM_HEAD_DIM, SSM_STATE)
    x = x.transpose(0, 1, 4, 2, 3).reshape(b, SSM_GROUPS, SSM_STATE, hg * SSM_HEAD_DIM)
    z = jnp.zeros_like(x)
    rows = [jnp.concatenate([x[:, g] if g2 == g else z[:, g] for g2 in range(SSM_GROUPS)], axis=-1)
            for g in range(SSM_GROUPS)]
    return jnp.concatenate(rows, axis=1)


def _state_from_packed(st):
    b = st.shape[0]
    hg = SSM_HEADS // SSM_GROUPS
    per = hg * SSM_HEAD_DIM
    blocks = [st[:, g * SSM_STATE:(g + 1) * SSM_STATE, g * per:(g + 1) * per] for g in range(SSM_GROUPS)]
    x = jnp.stack(blocks, axis=1).reshape(b, SSM_GROUPS, SSM_STATE, hg, SSM_HEAD_DIM)
    return x.transpose(0, 1, 3, 4, 2).reshape(b, SSM_HEADS, SSM_HEAD_DIM, SSM_STATE)


def _tail_kernel(x_ref, o_ref, ys_ref, woa_ref, wob_ref, n2_ref, wup_ref, wdn_ref, fn_ref, y_ref, *, ff_chunk):
    h = x_ref[...] + _dot(o_ref[...], woa_ref[...]) + _dot(ys_ref[...], wob_ref[...])
    hn = _rms(h, n2_ref[...]).astype(BF16)
    ffn = None
    for c in range(0, D_FF, ff_chunk):
        u = jnp.maximum(_dot(hn, wup_ref[:, c:c + ff_chunk]), 0.0)
        d = _dot((u * u).astype(BF16), wdn_ref[c:c + ff_chunk, :])
        ffn = d if ffn is None else ffn + d
    y_ref[...] = _rms(h + ffn, fn_ref[...])


def _tail(x2d, o2d, ys2d, wo_a, wo_b, n2, wup, wdn, fnw, tm, ff_chunk=1024):
    t = x2d.shape[0]
    row = lambda n: pl.BlockSpec((tm, n), lambda i: (i, 0))
    single = lambda s: pl.BlockSpec(s, lambda i: (0, 0), pipeline_mode=pl.Buffered(1))
    return pl.pallas_call(
        functools.partial(_tail_kernel, ff_chunk=ff_chunk),
        grid=(t // tm,),
        in_specs=[row(D_MODEL), row(SB_WIDTH), row(SSM_WIDTH),
                  single(wo_a.shape), single(wo_b.shape), single((1, D_MODEL)),
                  single(wup.shape), single(wdn.shape), single((1, D_MODEL))],
        out_specs=row(D_MODEL),
        out_shape=jax.ShapeDtypeStruct((t, D_MODEL), F32),
        compiler_params=pltpu.CompilerParams(
            dimension_semantics=("parallel",), vmem_limit_bytes=VMEM_LIMIT),
        name="out_ffn",
    )(x2d, o2d, ys2d, wo_a, wo_b, n2, wup, wdn, fnw)


def _pick(n, prefs):
    for p in prefs:
        if n % p == 0:
            return p
    raise ValueError(f"no tile in {prefs} divides {n}")


def _layer(x, k_past, v_past, conv0, st0, w):
    b, l, _ = x.shape
    t = b * l
    tm = _pick(t, (512, 256, 128, 64))
    if l % tm == 0:
        q, kt, vt, ktb, vtb, z, xbc, dt = _in_proj(x, w["norm1"], w["w_rest"], w["w_kv_t"], tm)
    else:
        outs = _in_proj(x.reshape(1, t, D_MODEL), w["norm1"], w["w_rest"], w["w_kv_t"], tm)
        split_t = lambda a: jnp.moveaxis(a.reshape(a.shape[1:-1] + (b, l)), -2, 0)
        q, z, xbc, dt = (a.reshape(b, l, a.shape[-1]) for a in (outs[0], outs[5], outs[6], outs[7]))
        kt, vt, ktb, vtb = (split_t(a) for a in outs[1:5])
    tq = _pick(l, (256, 128, 64))
    o = _attention(q, ktb, vtb, k_past, v_past, tq, LANES, 2 if k_past is None else SB_WIDTH // LANES)
    q_ssd = _pick(l, (128, 64))
    ys, conv_new, st_new = _ssd(xbc, z, dt, w["conv_w"], w["conv_b"], w["dt_bias"], w["a_log"],
                                w["d_skip"], w["ssm_norm_w"], conv0, st0, q_ssd, 4 if l % (4 * q_ssd) == 0 else 1)
    y = _tail(x.reshape(t, D_MODEL), o.reshape(t, SB_WIDTH), ys.reshape(t, SSM_WIDTH),
              w["wo_a"], w["wo_b"], w["norm2"], w["w_up"], w["w_down"], w["final"], tm)
    return y.reshape(b, l, D_MODEL), kt, vt, conv_new, _state_from_packed(st_new)


def kernel(x_prompt, x_sample, cache_k, cache_v, state_conv, state_ssm, norm1_w, w_in, conv_w, conv_b,
           dt_bias, a_log, d_skip, ssm_norm_w, w_out, norm2_w, w_up, w_down, final_norm_w):
    assert w_in.shape[0] == 1, "one layer: the final norm is fused into the layer's last kernel"
    bp = x_prompt.shape[0]
    in_cols = w_in.shape[-1]
    w = {
        "norm1": norm1_w[0].reshape(1, D_MODEL),
        "w_rest": jnp.pad(jnp.concatenate([w_in[0][:, :SB_WIDTH], w_in[0][:, 3 * SB_WIDTH:]], axis=1),
                          ((0, 0), (0, DT_PAD - SSM_HEADS))).astype(BF16),
        "w_kv_t": w_in[0][:, SB_WIDTH:3 * SB_WIDTH].T.astype(BF16),
        "conv_w": conv_w[0], "conv_b": conv_b[0], "dt_bias": dt_bias[0], "a_log": a_log[0],
        "d_skip": d_skip[0], "ssm_norm_w": ssm_norm_w[0],
        "wo_a": w_out[0, :SB_WIDTH].astype(BF16), "wo_b": w_out[0, SB_WIDTH:].astype(BF16),
        "norm2": norm2_w[0].reshape(1, D_MODEL),
        "w_up": w_up[0].astype(BF16), "w_down": w_down[0].astype(BF16),
        "final": final_norm_w.reshape(1, D_MODEL),
    }
    assert in_cols == 3 * SB_WIDTH + SSM_WIDTH + CONV_CH + SSM_HEADS
    conv_zero = jnp.zeros((bp, CONV_WIDTH - 1, CONV_CH), F32)
    st_zero = jnp.zeros((bp, LANES, SSM_WIDTH), F32)
    yp, kp, vp, cp, sp = _layer(x_prompt, None, None, conv_zero, st_zero, w)
    to_t = lambda a: jnp.transpose(a, (0, 2, 3, 1))
    from_t = lambda a: jnp.transpose(a, (0, 3, 1, 2))[None]
    ys, ks, vs, cs, ss = _layer(x_sample, to_t(cache_k[0]), to_t(cache_v[0]), state_conv[0],
                                _state_to_packed(state_ssm[0]), w)
    return (yp, ys, from_t(kp), from_t(vp), cp[None], sp[None], from_t(ks), from_t(vs), cs[None], ss[None])
```

```python
import functools

import numpy as np
import jax
import jax.numpy as jnp
from jax import lax
from jax.experimental import pallas as pl
from jax.experimental.pallas import tpu as pltpu

F32 = jnp.float32
BF16 = jnp.bfloat16

D_MODEL = 1024
SB_HEADS = 8
SB_HEAD_DIM = 64
SB_WIDTH = SB_HEADS * SB_HEAD_DIM
SSM_HEADS = 8
SSM_HEAD_DIM = 64
SSM_WIDTH = SSM_HEADS * SSM_HEAD_DIM
SSM_GROUPS = 2
SSM_STATE = 64
CONV_WIDTH = 4
CONV_CH = SSM_WIDTH + 2 * SSM_GROUPS * SSM_STATE
D_FF = 4 * D_MODEL
EPS = 1e-5

LANES = 128
CONV_PAD = 8
DT_PAD = LANES
VMEM_LIMIT = 56 * 1024 * 1024
EXIT_LOG = -105.0


def _dot(a, b):
    return jnp.dot(a, b, preferred_element_type=F32)


def _dot_nt(a, b):
    return lax.dot_general(a, b, (((1,), (1,)), ((), ())), preferred_element_type=F32)


def _dot_tn(a, b):
    return lax.dot_general(a, b, (((0,), (0,)), ((), ())), preferred_element_type=F32)


def _split(x, pieces):
    out = []
    r = x
    for i in range(pieces):
        p = r.astype(BF16)
        out.append(p)
        if i + 1 < pieces:
            r = r - p.astype(F32)
    return out


def _rms(x, w):
    return x * lax.rsqrt(jnp.mean(x * x, axis=-1, keepdims=True) + EPS) * w


def _softplus(x):
    return jnp.maximum(x, 0.0) + jnp.log1p(jnp.exp(-jnp.abs(x)))


def _silu(x):
    return x / (1.0 + jnp.exp(-x))


def _const_spec(shape):
    nd = len(shape)
    return pl.BlockSpec(shape, lambda *_: (0,) * nd)


def _in_proj_kernel(x_ref, nw_ref, w_ref, wkv_ref, q_ref, kt_ref, vt_ref, ktb_ref, vtb_ref, z_ref, xbc_ref, dt_ref):
    xn = _rms(x_ref[0], nw_ref[...]).astype(BF16)
    tm = xn.shape[0]
    c = 0
    q_ref[0] = (_dot(xn, w_ref[:, c:c + SB_WIDTH]) * (SB_HEAD_DIM ** -0.5)).astype(BF16)
    c += SB_WIDTH
    for r, (t_ref, tb_ref) in enumerate(((kt_ref, ktb_ref), (vt_ref, vtb_ref))):
        t = _dot_nt(wkv_ref[r * SB_WIDTH:(r + 1) * SB_WIDTH, :], xn)
        t_ref[0] = t.reshape(SB_HEADS, SB_HEAD_DIM, tm)
        tb_ref[0] = t.astype(BF16)
    z_ref[0] = _dot(xn, w_ref[:, c:c + SSM_WIDTH])
    c += SSM_WIDTH
    xbc_ref[0] = _dot(xn, w_ref[:, c:c + CONV_CH])
    c += CONV_CH
    dt_ref[0] = _dot(xn, w_ref[:, c:c + DT_PAD])


def _in_proj(x, norm_w, w_rest, w_kv_t, tm):
    b, l, _ = x.shape
    row = lambda n: pl.BlockSpec((1, tm, n), lambda i, j: (i, j, 0))
    outs = [
        ((l, SB_WIDTH), BF16, row(SB_WIDTH)),
        ((SB_HEADS, SB_HEAD_DIM, l), F32, pl.BlockSpec((1, SB_HEADS, SB_HEAD_DIM, tm), lambda i, j: (i, 0, 0, j))),
        ((SB_HEADS, SB_HEAD_DIM, l), F32, pl.BlockSpec((1, SB_HEADS, SB_HEAD_DIM, tm), lambda i, j: (i, 0, 0, j))),
        ((SB_WIDTH, l), BF16, pl.BlockSpec((1, SB_WIDTH, tm), lambda i, j: (i, 0, j))),
        ((SB_WIDTH, l), BF16, pl.BlockSpec((1, SB_WIDTH, tm), lambda i, j: (i, 0, j))),
        ((l, SSM_WIDTH), F32, row(SSM_WIDTH)), ((l, CONV_CH), F32, row(CONV_CH)), ((l, DT_PAD), F32, row(DT_PAD)),
    ]
    return pl.pallas_call(
        _in_proj_kernel,
        grid=(b, l // tm),
        in_specs=[row(D_MODEL), _const_spec((1, D_MODEL)), _const_spec(w_rest.shape), _const_spec(w_kv_t.shape)],
        out_specs=[spec for _, _, spec in outs],
        out_shape=[jax.ShapeDtypeStruct((b,) + shape, d) for shape, d, _ in outs],
        compiler_params=pltpu.CompilerParams(
            dimension_semantics=("parallel", "parallel"), vmem_limit_bytes=VMEM_LIMIT),
        name="in_proj",
    )(x, norm_w, w_rest, w_kv_t)


def _cum_matrix(tk):
    j = np.arange(tk)[:, None]
    s = np.arange(tk)[None, :]
    later = -(j > s).astype(np.float32)
    total = -np.ones((tk, LANES), np.float32)
    half = np.concatenate([later, total], axis=1)
    return jnp.asarray(np.concatenate([half, half], axis=0), dtype=BF16)


def _attn_kernel(*refs, tq, tk, tkd, n_past, nq, hps):
    if n_past:
        q_ref, kn_ref, vn_ref, kp_ref, vp_ref, ud_ref, uo_ref, o_ref = refs
    else:
        q_ref, kn_ref, vn_ref, ud_ref, uo_ref, o_ref = refs
    lane = lax.broadcasted_iota(jnp.int32, (1, LANES), 1)
    lo_lanes = lane < SB_HEAD_DIM
    lo_rows = lax.broadcasted_iota(jnp.int32, (LANES, 1), 0) < SB_HEAD_DIM
    n_fast = 2

    def logits(q_heads, specs, pairs):
        out = [None] * len(specs)
        for i, j in pairs:
            (kt_i, _, _, _, _, (i0, i1)), (kt_j, _, _, width, _, (j0, j1)) = specs[i], specs[j]
            u0, u1 = min(i0, j0), max(i1, j1)
            both = [_dot(q_heads[h][u0:u1], jnp.concatenate([kt_j, kt_i], axis=1)) for h in range(2)]
            out[j] = [z[j0 - u0:j1 - u0, :width] for z in both]
            out[i] = [z[i0 - u0:i1 - u0, width:] for z in both]
        for n, (kt_blk, _, _, _, _, (r0, r1)) in enumerate(specs):
            if out[n] is None:
                out[n] = [_dot(q_heads[h][r0:r1], kt_blk) for h in range(2)]
        return out

    def blocks(groups):
        zs = [logits(q_heads, specs, pairs) for q_heads, specs, _, pairs in groups]
        mids = []
        for (_, specs, _, _), zg in zip(groups, zs):
            mg = []
            for (_, _, u_ref, width, visible, _), zb in zip(specs, zg):
                mid = []
                for z in zb:
                    sp = jnp.maximum(z, 0.0) + jnp.log(1.0 + jnp.exp(-jnp.abs(z)))
                    spm = sp if visible is None else jnp.where(visible, sp, 0.0)
                    hi, lo = _split(spm, 2)
                    cs = _dot(jnp.concatenate([hi, lo], axis=1), u_ref[...])
                    mid.append((z - sp, cs))
                mg.append(mid)
            mids.append(mg)
        states = []
        for (_, specs, state, _), mg in zip(groups, mids):
            for (_, vt_blk, _, width, visible, (r0, r1)), mid in zip(specs, mg):
                new_carries = []
                probs = []
                for h, (log_beta, cs) in enumerate(mid):
                    carry = state[h][r0:r1]
                    p = jnp.exp(log_beta + cs[:, :width] + carry[:, :width])
                    if visible is not None:
                        p = jnp.where(visible, p, 0.0)
                    probs.append(p.astype(BF16))
                    new_carries.append(carry + cs[:, width:])
                vz = jnp.zeros_like(vt_blk)
                vcat = jnp.concatenate([jnp.where(lo_rows, vt_blk, vz), jnp.where(lo_rows, vz, vt_blk)], axis=1)
                acc = state[2][r0:r1] + _dot_nt(jnp.concatenate(probs, axis=1), vcat)
                out = (new_carries[0], new_carries[1], acc)
                if (r0, r1) != (0, tq):
                    out = tuple(jnp.concatenate(([old[:r0]] if r0 else []) + [new] + ([old[r1:]] if r1 < tq else []),
                                                axis=0) for old, new in zip(state, out))
                state = out
            states.append(state)
        return states

    def live(state, rows):
        return jnp.max(jnp.maximum(state[0][rows[0]:rows[1]], state[1][rows[0]:rows[1]])) > EXIT_LOG

    def kv_block(k_ref, v_ref, hp, start, size):
        if len(k_ref.shape) == 3:
            rows = slice(hp * LANES, (hp + 1) * LANES)
            return k_ref[0, rows, pl.ds(start, size)], v_ref[0, rows, pl.ds(start, size)]
        return tuple(r[0, 2 * hp:2 * hp + 2, :, pl.ds(start, size)].reshape(LANES, size).astype(BF16)
                     for r in (k_ref, v_ref))

    def sweep(q_heads, k_ref, v_ref, hp, n_blocks, skip, group, state):
        def cond(c):
            return jnp.logical_and(c[0] < (n_blocks - skip) // group, c[1])

        def body(c):
            i, _, st = c
            specs = []
            for g in range(group):
                start = pl.multiple_of((n_blocks - 1 - skip - (i * group + g)) * tk, tk)
                specs.append(kv_block(k_ref, v_ref, hp, start, tk) + (uo_ref, tk, None, (0, tq)))
            st = blocks([(q_heads, specs, st, [(0, 1)] if group == 2 else [])])[0]
            return i + 1, live(st, (0, tq)), st

        return lax.while_loop(cond, body, (jnp.int32(0), live(state, (0, tq)), state))[2]

    def tile_head(hp, qi, with_new):
        q0 = qi * tq if isinstance(qi, int) else pl.multiple_of(qi * tq, tq)
        q2 = q_ref[0, pl.ds(q0, tq), hp * LANES:(hp + 1) * LANES]
        qz = jnp.zeros_like(q2)
        q_heads = (jnp.where(lo_lanes, q2, qz), jnp.where(lo_lanes, qz, q2))
        zeros = jnp.zeros((tq, LANES), F32)
        state = (zeros, zeros, zeros)
        specs = []
        for d in reversed(range(tq // tkd)):
            rows = tq - d * tkd
            visible = (lax.broadcasted_iota(jnp.int32, (rows, tkd), 1)
                       < lax.broadcasted_iota(jnp.int32, (rows, tkd), 0))
            specs.append(kv_block(kn_ref, vn_ref, hp, q0 + d * tkd, tkd) + (ud_ref, tkd, visible, (d * tkd, tq)))
        n_diag = len(specs)
        pairs = [(0, 1)] if n_diag == 2 else []
        late = []
        if with_new:
            for j in range(n_fast):
                kv = kv_block(kn_ref, vn_ref, hp, pl.multiple_of(q0 - (j + 1) * tk, tk), tk)
                rows_j = max(tq - tk * j - tk // 2, tk // 2)
                specs.append(kv + (uo_ref, tk, None, (0, rows_j)))
                if rows_j < tq:
                    late.append(kv + (uo_ref, tk, None, (rows_j, tq)))
            pairs.append((n_diag, n_diag + 1))
        elif n_past:
            for j in range(n_fast):
                specs.append(kv_block(kp_ref, vp_ref, hp, (n_past - 1 - j) * tk, tk) + (uo_ref, tk, None, (0, tq)))
            pairs.append((n_diag, n_diag + 1))
        return q0, q_heads, specs, pairs, late, state

    def tile_rest(hp, qi, q_heads, late, state, with_new):
        for spec in late:
            state = lax.cond(live(state, spec[5]), lambda st, spec=spec: blocks([(q_heads, [spec], st, [])])[0],
                             lambda st: st, state)
        if with_new:
            state = sweep(q_heads, kn_ref, vn_ref, hp, qi * (tq // tk), n_fast, tq // tk, state)
        if n_past:
            state = sweep(q_heads, kp_ref, vp_ref, hp, n_past, 0 if with_new else n_fast, 2, state)
        return state

    def tiles(qi, with_new):
        heads = [tile_head(hp, qi, with_new) for hp in range(hps)]
        states = blocks([(q_heads, specs, state, pairs) for _, q_heads, specs, pairs, _, state in heads])
        if with_new or n_past:
            def rest(sts):
                return [tile_rest(hp, qi, q_heads, late, st, with_new)
                        for hp, ((_, q_heads, _, _, late, _), st) in enumerate(zip(heads, sts))]

            top = functools.reduce(jnp.maximum, [c for st in states for c in st[:2]])
            states = lax.cond(jnp.max(top) > EXIT_LOG, rest, lambda sts: sts, states)
        for hp, ((q0, _, _, _, _, _), state) in enumerate(zip(heads, states)):
            o_ref[0, pl.ds(q0, tq), hp * LANES:(hp + 1) * LANES] = state[2].astype(BF16)

    tiles(0, False)
    if nq > 1:
        def tile_body(qi, c):
            tiles(qi, True)
            return c

        lax.fori_loop(1, nq, tile_body, 0)


def _attention(q, kn, vn, kp, vp, tq, tk, hps):
    b, l, _ = q.shape
    tkd = min(tk, tq)
    nq = l // tq
    n_past = 0 if kp is None else kp.shape[3] // tk
    assert (nq == 1 or tq == 2 * tk) and (n_past == 0 or (nq == 1 and n_past % 2 == 0 and n_past >= 2))
    w = hps * LANES
    seq = lambda n: pl.BlockSpec((1, n, w), lambda bi, hi: (bi, 0, hi))
    seq_t = pl.BlockSpec((1, w, l), lambda bi, hi: (bi, hi, 0))
    ud, uo = _cum_matrix(tkd), _cum_matrix(tk)
    in_specs = [seq(l), seq_t, seq_t]
    args = [q, kn, vn]
    if n_past:
        assert w == SB_WIDTH
        past = pl.BlockSpec((1,) + kp.shape[1:], lambda bi, hi: (bi, 0, 0, 0))
        in_specs += [past, past]
        args += [kp, vp]
    in_specs += [_const_spec(ud.shape), _const_spec(uo.shape)]
    args += [ud, uo]
    return pl.pallas_call(
        functools.partial(_attn_kernel, tq=tq, tk=tk, tkd=tkd, n_past=n_past, nq=nq, hps=hps),
        grid=(b, SB_WIDTH // w),
        in_specs=in_specs,
        out_specs=seq(l),
        out_shape=jax.ShapeDtypeStruct((b, l, SB_WIDTH), BF16),
        compiler_params=pltpu.CompilerParams(
            dimension_semantics=("parallel", "parallel"), vmem_limit_bytes=VMEM_LIMIT),
        name="sb_attn",
    )(*args)


def _ssd_kernel(xbc_ref, z_ref, dt_ref, cw_ref, cb_ref, dtb_ref, alog_ref, dskip_ref, nw_ref,
                conv0_ref, st0_ref, ltri_ref, expand_ref, eye_ref, bmask_ref,
                y_ref, convout_ref, stout_ref, xp_scr, st_scr, *, q, cps):
    j = pl.program_id(1)
    rows = q * cps

    @pl.when(j == 0)
    def _():
        st_scr[...] = st0_ref[0]
        xp_scr[CONV_PAD - (CONV_WIDTH - 1):CONV_PAD, :] = conv0_ref[0]

    xp_scr[CONV_PAD:CONV_PAD + rows, :] = xbc_ref[0]
    base = CONV_PAD - (CONV_WIDTH - 1)
    xp = xp_scr[...]
    conv = xp * cw_ref[0:1, :]
    for i in range(1, CONV_WIDTH):
        conv = pltpu.roll(conv, 1, 0) + xp * cw_ref[i:i + 1, :]
    conv = conv[CONV_PAD:] + cb_ref[...]
    tail = xp_scr[base + rows:CONV_PAD + rows, :]
    xp_scr[base:CONV_PAD, :] = tail
    convout_ref[0] = tail
    xact = _silu(conv)
    xs_all = xact[:, :SSM_WIDTH]
    b2_all = xact[:, SSM_WIDTH:SSM_WIDTH + LANES].astype(BF16)
    c2_all = xact[:, SSM_WIDTH + LANES:].astype(BF16)
    dt_all = _softplus(dt_ref[0] + dtb_ref[...])
    a_all = dt_all * (-jnp.exp(alog_ref[...]))

    ltri = ltri_ref[...]
    expand = expand_ref[...]
    lane = lax.broadcasted_iota(jnp.int32, (1, LANES), 1)
    lo_lanes = lane < SSM_STATE
    causal = lax.broadcasted_iota(jnp.int32, (q, q), 0) >= lax.broadcasted_iota(jnp.int32, (q, q), 1)
    xz = jnp.zeros((q, LANES), BF16)
    st = st_scr[...]
    ys = []
    for c in range(cps):
        sl = slice(c * q, (c + 1) * q)
        xs, b2, c2b, dt = xs_all[sl], b2_all[sl], c2_all[sl], dt_all[sl]
        acum = _dot(ltri, jnp.concatenate(_split(a_all[sl], 3), axis=0))
        acum_p = jnp.concatenate(_split(acum, 3), axis=1)
        acum_t = _dot_nt(eye_ref[...], acum_p)
        acum_e = _dot(acum_p, expand)
        dt_e = _dot(jnp.concatenate(_split(dt, 3), axis=1), expand)
        xdt = xs * dt_e
        xdt_b = xdt.astype(BF16)
        end_e = acum_e[q - 1:q, :]
        xw = (xdt * jnp.exp(end_e - acum_e)).astype(BF16)
        y = _dot(c2b, st.astype(BF16)) * jnp.exp(acum_e)
        st = st * jnp.exp(end_e) + bmask_ref[...] * _dot_tn(b2, xw)
        cz = jnp.zeros_like(c2b)
        y_pairs = []
        for g in range(SSM_GROUPS):
            cg = jnp.where(lo_lanes, c2b, cz) if g == 0 else jnp.where(lo_lanes, cz, c2b)
            gram = _dot_nt(cg, b2)
            for pair in range(2):
                ms = []
                for hh in range(2):
                    h = 4 * g + 2 * pair + hh
                    d = acum[:, h:h + 1] - acum_t[h:h + 1, :]
                    decay = jnp.exp(jnp.where(causal, d, -1e30))
                    ms.append((gram * decay).astype(BF16))
                xd = xdt_b[:, (2 * g + pair) * LANES:(2 * g + pair + 1) * LANES]
                xcat = jnp.concatenate([jnp.where(lo_lanes, xd, xz), jnp.where(lo_lanes, xz, xd)], axis=0)
                y_pairs.append(_dot(jnp.concatenate(ms, axis=1), xcat))
        ys.append(y + jnp.concatenate(y_pairs, axis=1))
    st_scr[...] = st
    y = (ys[0] if cps == 1 else jnp.concatenate(ys, axis=0)) + dskip_ref[...] * xs_all
    y = y * _silu(z_ref[0])
    gw = SSM_WIDTH // SSM_GROUPS
    outs = []
    for g in range(SSM_GROUPS):
        yg = y[:, g * gw:(g + 1) * gw]
        outs.append(yg * lax.rsqrt(jnp.mean(yg * yg, axis=-1, keepdims=True) + EPS))
    y_ref[0] = (jnp.concatenate(outs, axis=1) * nw_ref[...]).astype(BF16)

    @pl.when(j == pl.num_programs(1) - 1)
    def _():
        stout_ref[0] = st


def _ssd(xbc, z, dt, conv_w, conv_b, dt_bias, a_log, d_skip, ssm_norm_w, conv0, st0, q, cps):
    b, l, _ = xbc.shape
    rows = q * cps
    tri = jnp.asarray(np.tile(np.tril(np.ones((q, q), np.float32)), (1, 3)), dtype=BF16)
    expand = np.zeros((LANES, SSM_WIDTH), np.float32)
    for h in range(SSM_HEADS):
        expand[h, h * SSM_HEAD_DIM:(h + 1) * SSM_HEAD_DIM] = 1.0
    expand = np.tile(expand, (3, 1))
    eye = jnp.asarray(np.tile(np.eye(LANES, dtype=np.float32), (1, 3)), dtype=BF16)
    pad = lambda v: jnp.pad(v.astype(F32), (0, LANES - v.shape[0])).reshape(1, LANES)
    seq = lambda n: pl.BlockSpec((1, rows, n), lambda bi, ji: (bi, ji, 0))
    per_b = lambda s: pl.BlockSpec((1,) + s, lambda bi, ji: (bi, 0, 0))
    consts = [
        conv_w.astype(F32), conv_b.astype(F32).reshape(1, CONV_CH), pad(dt_bias), pad(a_log),
        jnp.repeat(d_skip.astype(F32), SSM_HEAD_DIM).reshape(1, SSM_WIDTH),
        ssm_norm_w.astype(F32).reshape(1, SSM_WIDTH),
    ]
    mats = [tri, jnp.asarray(expand, dtype=BF16), eye, _state_block_mask()]
    return pl.pallas_call(
        functools.partial(_ssd_kernel, q=q, cps=cps),
        grid=(b, l // rows),
        in_specs=[seq(CONV_CH), seq(SSM_WIDTH), seq(DT_PAD)]
        + [_const_spec(c.shape) for c in consts]
        + [per_b((CONV_WIDTH - 1, CONV_CH)), per_b((LANES, SSM_WIDTH))]
        + [_const_spec(m.shape) for m in mats],
        out_specs=[seq(SSM_WIDTH), per_b((CONV_WIDTH - 1, CONV_CH)), per_b((LANES, SSM_WIDTH))],
        out_shape=[
            jax.ShapeDtypeStruct((b, l, SSM_WIDTH), BF16),
            jax.ShapeDtypeStruct((b, CONV_WIDTH - 1, CONV_CH), F32),
            jax.ShapeDtypeStruct((b, LANES, SSM_WIDTH), F32),
        ],
        scratch_shapes=[
            pltpu.VMEM((CONV_PAD + rows, CONV_CH), F32),
            pltpu.VMEM((LANES, SSM_WIDTH), F32),
        ],
        compiler_params=pltpu.CompilerParams(
            dimension_semantics=("parallel", "arbitrary"), vmem_limit_bytes=VMEM_LIMIT),
        name="conv_ssd",
    )(xbc, z, dt, *consts, conv0, st0, *mats)


def _state_block_mask():
    m = np.zeros((LANES, SSM_WIDTH), np.float32)
    per = SSM_WIDTH // SSM_GROUPS
    for g in range(SSM_GROUPS):
        m[g * SSM_STATE:(g + 1) * SSM_STATE, g * per:(g + 1) * per] = 1.0
    return jnp.asarray(m)


def _state_to_packed(h):
    b = h.shape[0]
    hg = SSM_HEADS // SSM_GROUPS
    x = h.astype(F32).reshape(b, SSM_GROUPS, hg, SSM_HEAD_DIM, SSM_STATE)
    x = x.transpose(0, 1, 4, 2, 3).reshape(b, SSM_GROUPS, SSM_STATE, hg * SSM_HEAD_DIM)
    z = jnp.zeros_like(x)
    rows = [jnp.concatenate([x[:, g] if g2 == g else z[:, g] for g2 in range(SSM_GROUPS)], axis=-1)
            for g in range(SSM_GROUPS)]
    return jnp.concatenate(rows, axis=1)


def _state_from_packed(st):
    b = st.shape[0]
    hg = SSM_HEADS // SSM_GROUPS
    per = hg * SSM_HEAD_DIM
    blocks = [st[:, g * SSM_STATE:(g + 1) * SSM_STATE, g * per:(g + 1) * per] for g in range(SSM_GROUPS)]
    x = jnp.stack(blocks, axis=1).reshape(b, SSM_GROUPS, SSM_STATE, hg, SSM_HEAD_DIM)
    return x.transpose(0, 1, 3, 4, 2).reshape(b, SSM_HEADS, SSM_HEAD_DIM, SSM_STATE)


def _tail_kernel(x_ref, o_ref, ys_ref, woa_ref, wob_ref, n2_ref, wup_ref, wdn_ref, fn_ref, y_ref, *, ff_chunk):
    h = x_ref[...] + _dot(o_ref[...], woa_ref[...]) + _dot(ys_ref[...], wob_ref[...])
    hn = _rms(h, n2_ref[...]).astype(BF16)
    ffn = None
    for c in range(0, D_FF, ff_chunk):
        u = jnp.maximum(_dot(hn, wup_ref[:, c:c + ff_chunk]), 0.0)
        d = _dot((u * u).astype(BF16), wdn_ref[c:c + ff_chunk, :])
        ffn = d if ffn is None else ffn + d
    y_ref[...] = _rms(h + ffn, fn_ref[...])


def _tail(x2d, o2d, ys2d, wo_a, wo_b, n2, wup, wdn, fnw, tm, ff_chunk=1024):
    t = x2d.shape[0]
    row = lambda n: pl.BlockSpec((tm, n), lambda i: (i, 0))
    single = lambda s: pl.BlockSpec(s, lambda i: (0, 0), pipeline_mode=pl.Buffered(1))
    return pl.pallas_call(
        functools.partial(_tail_kernel, ff_chunk=ff_chunk),
        grid=(t // tm,),
        in_specs=[row(D_MODEL), row(SB_WIDTH), row(SSM_WIDTH),
                  single(wo_a.shape), single(wo_b.shape), single((1, D_MODEL)),
                  single(wup.shape), single(wdn.shape), single((1, D_MODEL))],
        out_specs=row(D_MODEL),
        out_shape=jax.ShapeDtypeStruct((t, D_MODEL), F32),
        compiler_params=pltpu.CompilerParams(
            dimension_semantics=("parallel",), vmem_limit_bytes=VMEM_LIMIT),
        name="out_ffn",
    )(x2d, o2d, ys2d, wo_a, wo_b, n2, wup, wdn, fnw)


def _pick(n, prefs):
    for p in prefs:
        if n % p == 0:
            return p
    raise ValueError(f"no tile in {prefs} divides {n}")


def _layer(x, k_past, v_past, conv0, st0, w):
    b, l, _ = x.shape
    t = b * l
    tm = _pick(t, (512, 256, 128, 64))
    if l % tm == 0:
        q, kt, vt, ktb, vtb, z, xbc, dt = _in_proj(x, w["norm1"], w["w_rest"], w["w_kv_t"], tm)
    else:
        outs = _in_proj(x.reshape(1, t, D_MODEL), w["norm1"], w["w_rest"], w["w_kv_t"], tm)
        split_t = lambda a: jnp.moveaxis(a.reshape(a.shape[1:-1] + (b, l)), -2, 0)
        q, z, xbc, dt = (a.reshape(b, l, a.shape[-1]) for a in (outs[0], outs[5], outs[6], outs[7]))
        kt, vt, ktb, vtb = (split_t(a) for a in outs[1:5])
    tq = _pick(l, (256, 128, 64))
    o = _attention(q, ktb, vtb, k_past, v_past, tq, LANES, SB_WIDTH // LANES)
    q_ssd = _pick(l, (128, 64))
    ys, conv_new, st_new = _ssd(xbc, z, dt, w["conv_w"], w["conv_b"], w["dt_bias"], w["a_log"],
                                w["d_skip"], w["ssm_norm_w"], conv0, st0, q_ssd, 8 if l % (8 * q_ssd) == 0 else 1)
    y = _tail(x.reshape(t, D_MODEL), o.reshape(t, SB_WIDTH), ys.reshape(t, SSM_WIDTH),
              w["wo_a"], w["wo_b"], w["norm2"], w["w_up"], w["w_down"], w["final"], tm)
    return y.reshape(b, l, D_MODEL), kt, vt, conv_new, _state_from_packed(st_new)


def kernel(x_prompt, x_sample, cache_k, cache_v, state_conv, state_ssm, norm1_w, w_in, conv_w, conv_b,
           dt_bias, a_log, d_skip, ssm_norm_w, w_out, norm2_w, w_up, w_down, final_norm_w):
    assert w_in.shape[0] == 1, "one layer: the final norm is fused into the layer's last kernel"
    bp = x_prompt.shape[0]
    in_cols = w_in.shape[-1]
    w = {
        "norm1": norm1_w[0].reshape(1, D_MODEL),
        "w_rest": jnp.pad(jnp.concatenate([w_in[0][:, :SB_WIDTH], w_in[0][:, 3 * SB_WIDTH:]], axis=1),
                          ((0, 0), (0, DT_PAD - SSM_HEADS))).astype(BF16),
        "w_kv_t": w_in[0][:, SB_WIDTH:3 * SB_WIDTH].T.astype(BF16),
        "conv_w": conv_w[0], "conv_b": conv_b[0], "dt_bias": dt_bias[0], "a_log": a_log[0],
        "d_skip": d_skip[0], "ssm_norm_w": ssm_norm_w[0],
        "wo_a": w_out[0, :SB_WIDTH].astype(BF16), "wo_b": w_out[0, SB_WIDTH:].astype(BF16),
        "norm2": norm2_w[0].reshape(1, D_MODEL),
        "w_up": w_up[0].astype(BF16), "w_down": w_down[0].astype(BF16),
        "final": final_norm_w.reshape(1, D_MODEL),
    }
    assert in_cols == 3 * SB_WIDTH + SSM_WIDTH + CONV_CH + SSM_HEADS
    conv_zero = jnp.zeros((bp, CONV_WIDTH - 1, CONV_CH), F32)
    st_zero = jnp.zeros((bp, LANES, SSM_WIDTH), F32)
    yp, kp, vp, cp, sp = _layer(x_prompt, None, None, conv_zero, st_zero, w)
    to_t = lambda a: jnp.transpose(a, (0, 2, 3, 1))
    from_t = lambda a: jnp.transpose(a, (0, 3, 1, 2))[None]
    ys, ks, vs, cs, ss = _layer(x_sample, to_t(cache_k[0]), to_t(cache_v[0]), state_conv[0],
                                _state_to_packed(state_ssm[0]), w)
    return (yp, ys, from_t(kp), from_t(vp), cp[None], sp[None], from_t(ks), from_t(vs), cs[None], ss[None])
```

```python
import functools

import numpy as np
import jax
import jax.numpy as jnp
from jax import lax
from jax.experimental import pallas as pl
from jax.experimental.pallas import tpu as pltpu

F32 = jnp.float32
BF16 = jnp.bfloat16

D_MODEL = 1024
SB_HEADS = 8
SB_HEAD_DIM = 64
SB_WIDTH = SB_HEADS * SB_HEAD_DIM
SSM_HEADS = 8
SSM_HEAD_DIM = 64
SSM_WIDTH = SSM_HEADS * SSM_HEAD_DIM
SSM_GROUPS = 2
SSM_STATE = 64
CONV_WIDTH = 4
CONV_CH = SSM_WIDTH + 2 * SSM_GROUPS * SSM_STATE
D_FF = 4 * D_MODEL
EPS = 1e-5

LANES = 128
CONV_PAD = 8
DT_PAD = LANES
VMEM_LIMIT = 56 * 1024 * 1024
EXIT_LOG = -105.0


def _dot(a, b):
    return jnp.dot(a, b, preferred_element_type=F32)


def _dot_nt(a, b):
    return lax.dot_general(a, b, (((1,), (1,)), ((), ())), preferred_element_type=F32)


def _dot_tn(a, b):
    return lax.dot_general(a, b, (((0,), (0,)), ((), ())), preferred_element_type=F32)


def _split(x, pieces):
    out = []
    r = x
    for i in range(pieces):
        p = r.astype(BF16)
        out.append(p)
        if i + 1 < pieces:
            r = r - p.astype(F32)
    return out


def _rms(x, w):
    return x * lax.rsqrt(jnp.mean(x * x, axis=-1, keepdims=True) + EPS) * w


def _softplus(x):
    return jnp.maximum(x, 0.0) + jnp.log1p(jnp.exp(-jnp.abs(x)))


def _silu(x):
    return x / (1.0 + jnp.exp(-x))


def _const_spec(shape):
    nd = len(shape)
    return pl.BlockSpec(shape, lambda *_: (0,) * nd)


def _in_proj_kernel(x_ref, nw_ref, w_ref, wkv_ref, q_ref, kt_ref, vt_ref, ktb_ref, vtb_ref, z_ref, xbc_ref, dt_ref):
    xn = _rms(x_ref[0], nw_ref[...]).astype(BF16)
    tm = xn.shape[0]
    c = 0
    q_ref[0] = (_dot(xn, w_ref[:, c:c + SB_WIDTH]) * (SB_HEAD_DIM ** -0.5)).astype(BF16)
    c += SB_WIDTH
    for r, (t_ref, tb_ref) in enumerate(((kt_ref, ktb_ref), (vt_ref, vtb_ref))):
        t = _dot_nt(wkv_ref[r * SB_WIDTH:(r + 1) * SB_WIDTH, :], xn)
        t_ref[0] = t.reshape(SB_HEADS, SB_HEAD_DIM, tm)
        tb_ref[0] = t.astype(BF16)
    z_ref[0] = _dot(xn, w_ref[:, c:c + SSM_WIDTH])
    c += SSM_WIDTH
    xbc_ref[0] = _dot(xn, w_ref[:, c:c + CONV_CH])
    c += CONV_CH
    dt_ref[0] = _dot(xn, w_ref[:, c:c + DT_PAD])


def _in_proj(x, norm_w, w_rest, w_kv_t, tm):
    b, l, _ = x.shape
    row = lambda n: pl.BlockSpec((1, tm, n), lambda i, j: (i, j, 0))
    outs = [
        ((l, SB_WIDTH), BF16, row(SB_WIDTH)),
        ((SB_HEADS, SB_HEAD_DIM, l), F32, pl.BlockSpec((1, SB_HEADS, SB_HEAD_DIM, tm), lambda i, j: (i, 0, 0, j))),
        ((SB_HEADS, SB_HEAD_DIM, l), F32, pl.BlockSpec((1, SB_HEADS, SB_HEAD_DIM, tm), lambda i, j: (i, 0, 0, j))),
        ((SB_WIDTH, l), BF16, pl.BlockSpec((1, SB_WIDTH, tm), lambda i, j: (i, 0, j))),
        ((SB_WIDTH, l), BF16, pl.BlockSpec((1, SB_WIDTH, tm), lambda i, j: (i, 0, j))),
        ((l, SSM_WIDTH), F32, row(SSM_WIDTH)), ((l, CONV_CH), F32, row(CONV_CH)), ((l, DT_PAD), F32, row(DT_PAD)),
    ]
    return pl.pallas_call(
        _in_proj_kernel,
        grid=(b, l // tm),
        in_specs=[row(D_MODEL), _const_spec((1, D_MODEL)), _const_spec(w_rest.shape), _const_spec(w_kv_t.shape)],
        out_specs=[spec for _, _, spec in outs],
        out_shape=[jax.ShapeDtypeStruct((b,) + shape, d) for shape, d, _ in outs],
        compiler_params=pltpu.CompilerParams(
            dimension_semantics=("parallel", "parallel"), vmem_limit_bytes=VMEM_LIMIT),
        name="in_proj",
    )(x, norm_w, w_rest, w_kv_t)


def _cum_matrix(tk):
    j = np.arange(tk)[:, None]
    s = np.arange(tk)[None, :]
    later = -(j > s).astype(np.float32)
    total = -np.ones((tk, LANES), np.float32)
    half = np.concatenate([later, total], axis=1)
    return jnp.asarray(np.concatenate([half, half], axis=0), dtype=BF16)


def _attn_kernel(*refs, tq, tk, tkd, n_past, nq, hps):
    if n_past:
        q_ref, kn_ref, vn_ref, kp_ref, vp_ref, ud_ref, uo_ref, o_ref = refs
    else:
        q_ref, kn_ref, vn_ref, ud_ref, uo_ref, o_ref = refs
    lane = lax.broadcasted_iota(jnp.int32, (1, LANES), 1)
    lo_lanes = lane < SB_HEAD_DIM
    lo_rows = lax.broadcasted_iota(jnp.int32, (LANES, 1), 0) < SB_HEAD_DIM
    n_fast = 2

    def logits(q_heads, specs, pairs):
        out = [None] * len(specs)
        for i, j in pairs:
            (kt_i, _, _, _, _, (i0, i1)), (kt_j, _, _, width, _, (j0, j1)) = specs[i], specs[j]
            u0, u1 = min(i0, j0), max(i1, j1)
            both = [_dot(q_heads[h][u0:u1], jnp.concatenate([kt_j, kt_i], axis=1)) for h in range(2)]
            out[j] = [z[j0 - u0:j1 - u0, :width] for z in both]
            out[i] = [z[i0 - u0:i1 - u0, width:] for z in both]
        for n, (kt_blk, _, _, _, _, (r0, r1)) in enumerate(specs):
            if out[n] is None:
                out[n] = [_dot(q_heads[h][r0:r1], kt_blk) for h in range(2)]
        return out

    def blocks(groups):
        zs = [logits(q_heads, specs, pairs) for q_heads, specs, _, pairs in groups]
        mids = []
        for (_, specs, _, _), zg in zip(groups, zs):
            mg = []
            for (_, _, u_ref, width, visible, _), zb in zip(specs, zg):
                mid = []
                for z in zb:
                    sp = jnp.maximum(z, 0.0) + jnp.log(1.0 + jnp.exp(-jnp.abs(z)))
                    spm = sp if visible is None else jnp.where(visible, sp, 0.0)
                    hi, lo = _split(spm, 2)
                    cs = _dot(jnp.concatenate([hi, lo], axis=1), u_ref[...])
                    mid.append((z - sp, cs))
                mg.append(mid)
            mids.append(mg)
        states = []
        for (_, specs, state, _), mg in zip(groups, mids):
            for (_, vt_blk, _, width, visible, (r0, r1)), mid in zip(specs, mg):
                new_carries = []
                probs = []
                for h, (log_beta, cs) in enumerate(mid):
                    carry = state[h][r0:r1]
                    p = jnp.exp(log_beta + cs[:, :width] + carry[:, :width])
                    if visible is not None:
                        p = jnp.where(visible, p, 0.0)
                    probs.append(p.astype(BF16))
                    new_carries.append(carry + cs[:, width:])
                vz = jnp.zeros_like(vt_blk)
                vcat = jnp.concatenate([jnp.where(lo_rows, vt_blk, vz), jnp.where(lo_rows, vz, vt_blk)], axis=1)
                acc = state[2][r0:r1] + _dot_nt(jnp.concatenate(probs, axis=1), vcat)
                out = (new_carries[0], new_carries[1], acc)
                if (r0, r1) != (0, tq):
                    out = tuple(jnp.concatenate(([old[:r0]] if r0 else []) + [new] + ([old[r1:]] if r1 < tq else []),
                                                axis=0) for old, new in zip(state, out))
                state = out
            states.append(state)
        return states

    def live(state, rows):
        return jnp.max(jnp.maximum(state[0][rows[0]:rows[1]], state[1][rows[0]:rows[1]])) > EXIT_LOG

    def kv_block(k_ref, v_ref, hp, start, size):
        if len(k_ref.shape) == 3:
            rows = slice(hp * LANES, (hp + 1) * LANES)
            return k_ref[0, rows, pl.ds(start, size)], v_ref[0, rows, pl.ds(start, size)]
        return tuple(r[0, 2 * hp:2 * hp + 2, :, pl.ds(start, size)].reshape(LANES, size).astype(BF16)
                     for r in (k_ref, v_ref))

    def sweep(q_heads, k_ref, v_ref, hp, n_blocks, skip, group, state):
        def cond(c):
            return jnp.logical_and(c[0] < (n_blocks - skip) // group, c[1])

        def body(c):
            i, _, st = c
            specs = []
            for g in range(group):
                start = pl.multiple_of((n_blocks - 1 - skip - (i * group + g)) * tk, tk)
                specs.append(kv_block(k_ref, v_ref, hp, start, tk) + (uo_ref, tk, None, (0, tq)))
            st = blocks([(q_heads, specs, st, [(0, 1)] if group == 2 else [])])[0]
            return i + 1, live(st, (0, tq)), st

        return lax.while_loop(cond, body, (jnp.int32(0), live(state, (0, tq)), state))[2]

    def tile_head(hp, qi, with_new):
        q0 = qi * tq if isinstance(qi, int) else pl.multiple_of(qi * tq, tq)
        q2 = q_ref[0, pl.ds(q0, tq), hp * LANES:(hp + 1) * LANES]
        qz = jnp.zeros_like(q2)
        q_heads = (jnp.where(lo_lanes, q2, qz), jnp.where(lo_lanes, qz, q2))
        zeros = jnp.zeros((tq, LANES), F32)
        state = (zeros, zeros, zeros)
        specs = []
        for d in reversed(range(tq // tkd)):
            rows = tq - d * tkd
            visible = (lax.broadcasted_iota(jnp.int32, (rows, tkd), 1)
                       < lax.broadcasted_iota(jnp.int32, (rows, tkd), 0))
            specs.append(kv_block(kn_ref, vn_ref, hp, q0 + d * tkd, tkd) + (ud_ref, tkd, visible, (d * tkd, tq)))
        n_diag = len(specs)
        pairs = [(0, 1)] if n_diag == 2 else []
        late = []
        if with_new:
            for j in range(n_fast):
                kv = kv_block(kn_ref, vn_ref, hp, pl.multiple_of(q0 - (j + 1) * tk, tk), tk)
                rows_j = max(tq - tk * j - tk // 2, tk // 2)
                specs.append(kv + (uo_ref, tk, None, (0, rows_j)))
                if rows_j < tq:
                    late.append(kv + (uo_ref, tk, None, (rows_j, tq)))
            pairs.append((n_diag, n_diag + 1))
        elif n_past:
            for j in range(n_fast):
                specs.append(kv_block(kp_ref, vp_ref, hp, (n_past - 1 - j) * tk, tk) + (uo_ref, tk, None, (0, tq)))
            pairs.append((n_diag, n_diag + 1))
        return q0, q_heads, specs, pairs, late, state

    def tile_rest(hp, qi, q_heads, late, state, with_new):
        for spec in late:
            state = lax.cond(live(state, spec[5]), lambda st, spec=spec: blocks([(q_heads, [spec], st, [])])[0],
                             lambda st: st, state)
        if with_new:
            state = sweep(q_heads, kn_ref, vn_ref, hp, qi * (tq // tk), n_fast, tq // tk, state)
        if n_past:
            state = sweep(q_heads, kp_ref, vp_ref, hp, n_past, 0 if with_new else n_fast, 2, state)
        return state

    def tiles(qi, with_new):
        heads = [tile_head(hp, qi, with_new) for hp in range(hps)]
        states = blocks([(q_heads, specs, state, pairs) for _, q_heads, specs, pairs, _, state in heads])
        if with_new or n_past:
            def rest(sts):
                return [tile_rest(hp, qi, q_heads, late, st, with_new)
                        for hp, ((_, q_heads, _, _, late, _), st) in enumerate(zip(heads, sts))]

            top = functools.reduce(jnp.maximum, [c for st in states for c in st[:2]])
            states = lax.cond(jnp.max(top) > EXIT_LOG, rest, lambda sts: sts, states)
        for hp, ((q0, _, _, _, _, _), state) in enumerate(zip(heads, states)):
            o_ref[0, pl.ds(q0, tq), hp * LANES:(hp + 1) * LANES] = state[2].astype(BF16)

    tiles(0, False)
    if nq > 1:
        def tile_body(qi, c):
            tiles(qi, True)
            return c

        lax.fori_loop(1, nq, tile_body, 0)


def _attention(q, kn, vn, kp, vp, tq, tk, hps):
    b, l, _ = q.shape
    tkd = min(tk, tq)
    nq = l // tq
    n_past = 0 if kp is None else kp.shape[3] // tk
    assert (nq == 1 or tq == 2 * tk) and (n_past == 0 or (nq == 1 and n_past % 2 == 0 and n_past >= 2))
    w = hps * LANES
    seq = lambda n: pl.BlockSpec((1, n, w), lambda bi, hi: (bi, 0, hi))
    seq_t = pl.BlockSpec((1, w, l), lambda bi, hi: (bi, hi, 0))
    ud, uo = _cum_matrix(tkd), _cum_matrix(tk)
    in_specs = [seq(l), seq_t, seq_t]
    args = [q, kn, vn]
    if n_past:
        assert w == SB_WIDTH
        past = pl.BlockSpec((1,) + kp.shape[1:], lambda bi, hi: (bi, 0, 0, 0))
        in_specs += [past, past]
        args += [kp, vp]
    in_specs += [_const_spec(ud.shape), _const_spec(uo.shape)]
    args += [ud, uo]
    return pl.pallas_call(
        functools.partial(_attn_kernel, tq=tq, tk=tk, tkd=tkd, n_past=n_past, nq=nq, hps=hps),
        grid=(b, SB_WIDTH // w),
        in_specs=in_specs,
        out_specs=seq(l),
        out_shape=jax.ShapeDtypeStruct((b, l, SB_WIDTH), BF16),
        compiler_params=pltpu.CompilerParams(
            dimension_semantics=("parallel", "parallel"), vmem_limit_bytes=VMEM_LIMIT),
        name="sb_attn",
    )(*args)


def _ssd_kernel(xbc_ref, z_ref, dt_ref, cw_ref, cb_ref, dtb_ref, alog_ref, dskip_ref, nw_ref,
                conv0_ref, st0_ref, ltri_ref, expand_ref, eye_ref, bmask_ref,
                y_ref, convout_ref, stout_ref, xp_scr, st_scr, *, q, cps):
    j = pl.program_id(1)
    rows = q * cps

    @pl.when(j == 0)
    def _():
        st_scr[...] = st0_ref[0]
        xp_scr[CONV_PAD - (CONV_WIDTH - 1):CONV_PAD, :] = conv0_ref[0]

    xp_scr[CONV_PAD:CONV_PAD + rows, :] = xbc_ref[0]
    base = CONV_PAD - (CONV_WIDTH - 1)
    xp = xp_scr[...]
    conv = xp * cw_ref[0:1, :]
    for i in range(1, CONV_WIDTH):
        conv = pltpu.roll(conv, 1, 0) + xp * cw_ref[i:i + 1, :]
    conv = conv[CONV_PAD:] + cb_ref[...]
    tail = xp_scr[base + rows:CONV_PAD + rows, :]
    xp_scr[base:CONV_PAD, :] = tail
    convout_ref[0] = tail
    xact = _silu(conv)
    xs_all = xact[:, :SSM_WIDTH]
    b2_all = xact[:, SSM_WIDTH:SSM_WIDTH + LANES].astype(BF16)
    c2_all = xact[:, SSM_WIDTH + LANES:].astype(BF16)
    dt_all = _softplus(dt_ref[0] + dtb_ref[...])
    a_all = dt_all * (-jnp.exp(alog_ref[...]))

    ltri = ltri_ref[...]
    expand = expand_ref[...]
    lane = lax.broadcasted_iota(jnp.int32, (1, LANES), 1)
    lo_lanes = lane < SSM_STATE
    causal = lax.broadcasted_iota(jnp.int32, (q, q), 0) >= lax.broadcasted_iota(jnp.int32, (q, q), 1)
    xz = jnp.zeros((q, LANES), BF16)
    sls = [slice(c * q, (c + 1) * q) for c in range(cps)]
    acum = [_dot(ltri, jnp.concatenate(_split(a_all[sl], 3), axis=0)) for sl in sls]
    acum_p = [jnp.concatenate(_split(a, 3), axis=1) for a in acum]
    acum_t = [_dot_nt(eye_ref[...], p) for p in acum_p]
    acum_e = [_dot(p, expand) for p in acum_p]
    dt_e = [_dot(jnp.concatenate(_split(dt_all[sl], 3), axis=1), expand) for sl in sls]
    grams = []
    for sl in sls:
        c2b, b2 = c2_all[sl], b2_all[sl]
        cz = jnp.zeros_like(c2b)
        grams.append([_dot_nt(jnp.where(lo_lanes, c2b, cz) if g == 0 else jnp.where(lo_lanes, cz, c2b), b2)
                      for g in range(SSM_GROUPS)])
    xdt = [xs_all[sl] * d for sl, d in zip(sls, dt_e)]
    st = st_scr[...]
    y_off = []
    for c, sl in enumerate(sls):
        end_e = acum_e[c][q - 1:q, :]
        xw = (xdt[c] * jnp.exp(end_e - acum_e[c])).astype(BF16)
        y_off.append(_dot(c2_all[sl], st.astype(BF16)) * jnp.exp(acum_e[c]))
        st = st * jnp.exp(end_e) + bmask_ref[...] * _dot_tn(b2_all[sl], xw)
    ys = []
    for c in range(cps):
        xdt_b = xdt[c].astype(BF16)
        y_pairs = []
        for g in range(SSM_GROUPS):
            for pair in range(2):
                ms = []
                for hh in range(2):
                    h = 4 * g + 2 * pair + hh
                    d = acum[c][:, h:h + 1] - acum_t[c][h:h + 1, :]
                    decay = jnp.exp(jnp.where(causal, d, -1e30))
                    ms.append((grams[c][g] * decay).astype(BF16))
                xd = xdt_b[:, (2 * g + pair) * LANES:(2 * g + pair + 1) * LANES]
                xcat = jnp.concatenate([jnp.where(lo_lanes, xd, xz), jnp.where(lo_lanes, xz, xd)], axis=0)
                y_pairs.append(_dot(jnp.concatenate(ms, axis=1), xcat))
        ys.append(y_off[c] + jnp.concatenate(y_pairs, axis=1))
    st_scr[...] = st
    y = (ys[0] if cps == 1 else jnp.concatenate(ys, axis=0)) + dskip_ref[...] * xs_all
    y = y * _silu(z_ref[0])
    gw = SSM_WIDTH // SSM_GROUPS
    outs = []
    for g in range(SSM_GROUPS):
        yg = y[:, g * gw:(g + 1) * gw]
        outs.append(yg * lax.rsqrt(jnp.mean(yg * yg, axis=-1, keepdims=True) + EPS))
    y_ref[0] = (jnp.concatenate(outs, axis=1) * nw_ref[...]).astype(BF16)

    @pl.when(j == pl.num_programs(1) - 1)
    def _():
        stout_ref[0] = st


def _ssd(xbc, z, dt, conv_w, conv_b, dt_bias, a_log, d_skip, ssm_norm_w, conv0, st0, q, cps):
    b, l, _ = xbc.shape
    rows = q * cps
    tri = jnp.asarray(np.tile(np.tril(np.ones((q, q), np.float32)), (1, 3)), dtype=BF16)
    expand = np.zeros((LANES, SSM_WIDTH), np.float32)
    for h in range(SSM_HEADS):
        expand[h, h * SSM_HEAD_DIM:(h + 1) * SSM_HEAD_DIM] = 1.0
    expand = np.tile(expand, (3, 1))
    eye = jnp.asarray(np.tile(np.eye(LANES, dtype=np.float32), (1, 3)), dtype=BF16)
    pad = lambda v: jnp.pad(v.astype(F32), (0, LANES - v.shape[0])).reshape(1, LANES)
    seq = lambda n: pl.BlockSpec((1, rows, n), lambda bi, ji: (bi, ji, 0))
    per_b = lambda s: pl.BlockSpec((1,) + s, lambda bi, ji: (bi, 0, 0))
    consts = [
        conv_w.astype(F32), conv_b.astype(F32).reshape(1, CONV_CH), pad(dt_bias), pad(a_log),
        jnp.repeat(d_skip.astype(F32), SSM_HEAD_DIM).reshape(1, SSM_WIDTH),
        ssm_norm_w.astype(F32).reshape(1, SSM_WIDTH),
    ]
    mats = [tri, jnp.asarray(expand, dtype=BF16), eye, _state_block_mask()]
    return pl.pallas_call(
        functools.partial(_ssd_kernel, q=q, cps=cps),
        grid=(b, l // rows),
        in_specs=[seq(CONV_CH), seq(SSM_WIDTH), seq(DT_PAD)]
        + [_const_spec(c.shape) for c in consts]
        + [per_b((CONV_WIDTH - 1, CONV_CH)), per_b((LANES, SSM_WIDTH))]
        + [_const_spec(m.shape) for m in mats],
        out_specs=[seq(SSM_WIDTH), per_b((CONV_WIDTH - 1, CONV_CH)), per_b((LANES, SSM_WIDTH))],
        out_shape=[
            jax.ShapeDtypeStruct((b, l, SSM_WIDTH), BF16),
            jax.ShapeDtypeStruct((b, CONV_WIDTH - 1, CONV_CH), F32),
            jax.ShapeDtypeStruct((b, LANES, SSM_WIDTH), F32),
        ],
        scratch_shapes=[
            pltpu.VMEM((CONV_PAD + rows, CONV_CH), F32),
            pltpu.VMEM((LANES, SSM_WIDTH), F32),
        ],
        compiler_params=pltpu.CompilerParams(
            dimension_semantics=("parallel", "arbitrary"), vmem_limit_bytes=VMEM_LIMIT),
        name="conv_ssd",
    )(xbc, z, dt, *consts, conv0, st0, *mats)


def _state_block_mask():
    m = np.zeros((LANES, SSM_WIDTH), np.float32)
    per = SSM_WIDTH // SSM_GROUPS
    for g in range(SSM_GROUPS):
        m[g * SSM_STATE:(g + 1) * SSM_STATE, g * per:(g + 1) * per] = 1.0
    return jnp.asarray(m)


def _state_to_packed(h):
    b = h.shape[0]
    hg = SSM_HEADS // SSM_GROUPS
    x = h.astype(F32).reshape(b, SSM_GROUPS, hg, SSM_HEAD_DIM, SSM_STATE)
    x = x.transpose(0, 1, 4, 2, 3).reshape(b, SSM_GROUPS, SSM_STATE, hg * SSM_HEAD_DIM)
    z = jnp.zeros_like(x)
    rows = [jnp.concatenate([x[:, g] if g2 == g else z[:, g] for g2 in range(SSM_GROUPS)], axis=-1)
            for g in range(SSM_GROUPS)]
    return jnp.concatenate(rows, axis=1)


def _state_from_packed(st):
    b = st.shape[0]
    hg = SSM_HEADS // SSM_GROUPS
    per = hg * SSM_HEAD_DIM
    blocks = [st[:, g * SSM_STATE:(g + 1) * SSM_STATE, g * per:(g + 1) * per] for g in range(SSM_GROUPS)]
    x = jnp.stack(blocks, axis=1).reshape(b, SSM_GROUPS, SSM_STATE, hg, SSM_HEAD_DIM)
    return x.transpose(0, 1, 3, 4, 2).reshape(b, SSM_HEADS, SSM_HEAD_DIM, SSM_STATE)


def _tail_kernel(x_ref, o_ref, ys_ref, woa_ref, wob_ref, n2_ref, wup_ref, wdn_ref, fn_ref, y_ref, *, ff_chunk):
    h = x_ref[...] + _dot(o_ref[...], woa_ref[...]) + _dot(ys_ref[...], wob_ref[...])
    hn = _rms(h, n2_ref[...]).astype(BF16)
    ffn = None
    for c in range(0, D_FF, ff_chunk):
        u = jnp.maximum(_dot(hn, wup_ref[:, c:c + ff_chunk]), 0.0)
        d = _dot((u * u).astype(BF16), wdn_ref[c:c + ff_chunk, :])
        ffn = d if ffn is None else ffn + d
    y_ref[...] = _rms(h + ffn, fn_ref[...])


def _tail(x2d, o2d, ys2d, wo_a, wo_b, n2, wup, wdn, fnw, tm, ff_chunk=1024):
    t = x2d.shape[0]
    row = lambda n: pl.BlockSpec((tm, n), lambda i: (i, 0))
    single = lambda s: pl.BlockSpec(s, lambda i: (0, 0), pipeline_mode=pl.Buffered(1))
    return pl.pallas_call(
        functools.partial(_tail_kernel, ff_chunk=ff_chunk),
        grid=(t // tm,),
        in_specs=[row(D_MODEL), row(SB_WIDTH), row(SSM_WIDTH),
                  single(wo_a.shape), single(wo_b.shape), single((1, D_MODEL)),
                  single(wup.shape), single(wdn.shape), single((1, D_MODEL))],
        out_specs=row(D_MODEL),
        out_shape=jax.ShapeDtypeStruct((t, D_MODEL), F32),
        compiler_params=pltpu.CompilerParams(
            dimension_semantics=("parallel",), vmem_limit_bytes=VMEM_LIMIT),
        name="out_ffn",
    )(x2d, o2d, ys2d, wo_a, wo_b, n2, wup, wdn, fnw)


def _pick(n, prefs):
    for p in prefs:
        if n % p == 0:
            return p
    raise ValueError(f"no tile in {prefs} divides {n}")


def _layer(x, k_past, v_past, conv0, st0, w):
    b, l, _ = x.shape
    t = b * l
    tm = _pick(t, (512, 256, 128, 64))
    if l % tm == 0:
        q, kt, vt, ktb, vtb, z, xbc, dt = _in_proj(x, w["norm1"], w["w_rest"], w["w_kv_t"], tm)
    else:
        outs = _in_proj(x.reshape(1, t, D_MODEL), w["norm1"], w["w_rest"], w["w_kv_t"], tm)
        split_t = lambda a: jnp.moveaxis(a.reshape(a.shape[1:-1] + (b, l)), -2, 0)
        q, z, xbc, dt = (a.reshape(b, l, a.shape[-1]) for a in (outs[0], outs[5], outs[6], outs[7]))
        kt, vt, ktb, vtb = (split_t(a) for a in outs[1:5])
    tq = _pick(l, (256, 128, 64))
    o = _attention(q, ktb, vtb, k_past, v_past, tq, LANES, SB_WIDTH // LANES)
    q_ssd = _pick(l, (128, 64))
    ys, conv_new, st_new = _ssd(xbc, z, dt, w["conv_w"], w["conv_b"], w["dt_bias"], w["a_log"],
                                w["d_skip"], w["ssm_norm_w"], conv0, st0, q_ssd, 8 if l % (8 * q_ssd) == 0 else 1)
    y = _tail(x.reshape(t, D_MODEL), o.reshape(t, SB_WIDTH), ys.reshape(t, SSM_WIDTH),
              w["wo_a"], w["wo_b"], w["norm2"], w["w_up"], w["w_down"], w["final"], tm)
    return y.reshape(b, l, D_MODEL), kt, vt, conv_new, _state_from_packed(st_new)


def kernel(x_prompt, x_sample, cache_k, cache_v, state_conv, state_ssm, norm1_w, w_in, conv_w, conv_b,
           dt_bias, a_log, d_skip, ssm_norm_w, w_out, norm2_w, w_up, w_down, final_norm_w):
    assert w_in.shape[0] == 1, "one layer: the final norm is fused into the layer's last kernel"
    bp = x_prompt.shape[0]
    in_cols = w_in.shape[-1]
    w = {
        "norm1": norm1_w[0].reshape(1, D_MODEL),
        "w_rest": jnp.pad(jnp.concatenate([w_in[0][:, :SB_WIDTH], w_in[0][:, 3 * SB_WIDTH:]], axis=1),
                          ((0, 0), (0, DT_PAD - SSM_HEADS))).astype(BF16),
        "w_kv_t": w_in[0][:, SB_WIDTH:3 * SB_WIDTH].T.astype(BF16),
        "conv_w": conv_w[0], "conv_b": conv_b[0], "dt_bias": dt_bias[0], "a_log": a_log[0],
        "d_skip": d_skip[0], "ssm_norm_w": ssm_norm_w[0],
        "wo_a": w_out[0, :SB_WIDTH].astype(BF16), "wo_b": w_out[0, SB_WIDTH:].astype(BF16),
        "norm2": norm2_w[0].reshape(1, D_MODEL),
        "w_up": w_up[0].astype(BF16), "w_down": w_down[0].astype(BF16),
        "final": final_norm_w.reshape(1, D_MODEL),
    }
    assert in_cols == 3 * SB_WIDTH + SSM_WIDTH + CONV_CH + SSM_HEADS
    conv_zero = jnp.zeros((bp, CONV_WIDTH - 1, CONV_CH), F32)
    st_zero = jnp.zeros((bp, LANES, SSM_WIDTH), F32)
    yp, kp, vp, cp, sp = _layer(x_prompt, None, None, conv_zero, st_zero, w)
    to_t = lambda a: jnp.transpose(a, (0, 2, 3, 1))
    from_t = lambda a: jnp.transpose(a, (0, 3, 1, 2))[None]
    ys, ks, vs, cs, ss = _layer(x_sample, to_t(cache_k[0]), to_t(cache_v[0]), state_conv[0],
                                _state_to_packed(state_ssm[0]), w)
    return (yp, ys, from_t(kp), from_t(vp), cp[None], sp[None], from_t(ks), from_t(vs), cs[None], ss[None])
```

```python
import functools

import numpy as np
import jax
import jax.numpy as jnp
from jax import lax
from jax.experimental import pallas as pl
from jax.experimental.pallas import tpu as pltpu

F32 = jnp.float32
BF16 = jnp.bfloat16

D_MODEL = 1024
SB_HEADS = 8
SB_HEAD_DIM = 64
SB_WIDTH = SB_HEADS * SB_HEAD_DIM
SSM_HEADS = 8
SSM_HEAD_DIM = 64
SSM_WIDTH = SSM_HEADS * SSM_HEAD_DIM
SSM_GROUPS = 2
SSM_STATE = 64
CONV_WIDTH = 4
CONV_CH = SSM_WIDTH + 2 * SSM_GROUPS * SSM_STATE
D_FF = 4 * D_MODEL
EPS = 1e-5

LANES = 128
CONV_PAD = 8
DT_PAD = LANES
VMEM_LIMIT = 56 * 1024 * 1024
EXIT_LOG = -105.0


def _dot(a, b):
    return jnp.dot(a, b, preferred_element_type=F32)


def _dot_nt(a, b):
    return lax.dot_general(a, b, (((1,), (1,)), ((), ())), preferred_element_type=F32)


def _dot_tn(a, b):
    return lax.dot_general(a, b, (((0,), (0,)), ((), ())), preferred_element_type=F32)


def _split(x, pieces):
    out = []
    r = x
    for i in range(pieces):
        p = r.astype(BF16)
        out.append(p)
        if i + 1 < pieces:
            r = r - p.astype(F32)
    return out


def _rms(x, w):
    return x * lax.rsqrt(jnp.mean(x * x, axis=-1, keepdims=True) + EPS) * w


def _softplus(x):
    return jnp.maximum(x, 0.0) + jnp.log1p(jnp.exp(-jnp.abs(x)))


def _silu(x):
    return x / (1.0 + jnp.exp(-x))


def _const_spec(shape):
    nd = len(shape)
    return pl.BlockSpec(shape, lambda *_: (0,) * nd)


def _in_proj_kernel(x_ref, nw_ref, w_ref, wkv_ref, q_ref, kt_ref, vt_ref, ktb_ref, vtb_ref, z_ref, xbc_ref, dt_ref):
    xn = _rms(x_ref[0], nw_ref[...]).astype(BF16)
    tm = xn.shape[0]
    c = 0
    q_ref[0] = (_dot(xn, w_ref[:, c:c + SB_WIDTH]) * (SB_HEAD_DIM ** -0.5)).astype(BF16)
    c += SB_WIDTH
    for r, (t_ref, tb_ref) in enumerate(((kt_ref, ktb_ref), (vt_ref, vtb_ref))):
        t = _dot_nt(wkv_ref[r * SB_WIDTH:(r + 1) * SB_WIDTH, :], xn)
        t_ref[0] = t.reshape(SB_HEADS, SB_HEAD_DIM, tm)
        tb_ref[0] = t.astype(BF16)
    z_ref[0] = _dot(xn, w_ref[:, c:c + SSM_WIDTH])
    c += SSM_WIDTH
    xbc_ref[0] = _dot(xn, w_ref[:, c:c + CONV_CH])
    c += CONV_CH
    dt_ref[0] = _dot(xn, w_ref[:, c:c + DT_PAD])


def _in_proj(x, norm_w, w_rest, w_kv_t, tm):
    b, l, _ = x.shape
    row = lambda n: pl.BlockSpec((1, tm, n), lambda i, j: (i, j, 0))
    outs = [
        ((l, SB_WIDTH), BF16, row(SB_WIDTH)),
        ((SB_HEADS, SB_HEAD_DIM, l), F32, pl.BlockSpec((1, SB_HEADS, SB_HEAD_DIM, tm), lambda i, j: (i, 0, 0, j))),
        ((SB_HEADS, SB_HEAD_DIM, l), F32, pl.BlockSpec((1, SB_HEADS, SB_HEAD_DIM, tm), lambda i, j: (i, 0, 0, j))),
        ((SB_WIDTH, l), BF16, pl.BlockSpec((1, SB_WIDTH, tm), lambda i, j: (i, 0, j))),
        ((SB_WIDTH, l), BF16, pl.BlockSpec((1, SB_WIDTH, tm), lambda i, j: (i, 0, j))),
        ((l, SSM_WIDTH), F32, row(SSM_WIDTH)), ((l, CONV_CH), F32, row(CONV_CH)), ((l, DT_PAD), F32, row(DT_PAD)),
    ]
    return pl.pallas_call(
        _in_proj_kernel,
        grid=(b, l // tm),
        in_specs=[row(D_MODEL), _const_spec((1, D_MODEL)), _const_spec(w_rest.shape), _const_spec(w_kv_t.shape)],
        out_specs=[spec for _, _, spec in outs],
        out_shape=[jax.ShapeDtypeStruct((b,) + shape, d) for shape, d, _ in outs],
        compiler_params=pltpu.CompilerParams(
            dimension_semantics=("parallel", "parallel"), vmem_limit_bytes=VMEM_LIMIT),
        name="in_proj",
    )(x, norm_w, w_rest, w_kv_t)


def _cum_matrix(tk):
    j = np.arange(tk)[:, None]
    s = np.arange(tk)[None, :]
    later = -(j > s).astype(np.float32)
    total = -np.ones((tk, LANES), np.float32)
    half = np.concatenate([later, total], axis=1)
    return jnp.asarray(np.concatenate([half, half], axis=0), dtype=BF16)


def _attn_kernel(*refs, tq, tk, tkd, n_past, nq, hps, bs):
    if n_past:
        q_ref, kn_ref, vn_ref, kp_ref, vp_ref, ud_ref, uo_ref, o_ref = refs
    else:
        q_ref, kn_ref, vn_ref, ud_ref, uo_ref, o_ref = refs
    lane = lax.broadcasted_iota(jnp.int32, (1, LANES), 1)
    lo_lanes = lane < SB_HEAD_DIM
    lo_rows = lax.broadcasted_iota(jnp.int32, (LANES, 1), 0) < SB_HEAD_DIM
    n_fast = 2

    def logits(q_heads, specs, pairs):
        out = [None] * len(specs)
        for i, j in pairs:
            (kt_i, _, _, _, _, (i0, i1)), (kt_j, _, _, width, _, (j0, j1)) = specs[i], specs[j]
            u0, u1 = min(i0, j0), max(i1, j1)
            both = [_dot(q_heads[h][u0:u1], jnp.concatenate([kt_j, kt_i], axis=1)) for h in range(2)]
            out[j] = [z[j0 - u0:j1 - u0, :width] for z in both]
            out[i] = [z[i0 - u0:i1 - u0, width:] for z in both]
        for n, (kt_blk, _, _, _, _, (r0, r1)) in enumerate(specs):
            if out[n] is None:
                out[n] = [_dot(q_heads[h][r0:r1], kt_blk) for h in range(2)]
        return out

    def blocks(groups):
        zs = [logits(q_heads, specs, pairs) for q_heads, specs, _, pairs in groups]
        mids = []
        for (_, specs, _, _), zg in zip(groups, zs):
            mg = []
            for (_, _, u_ref, width, visible, _), zb in zip(specs, zg):
                mid = []
                for z in zb:
                    sp = jnp.maximum(z, 0.0) + jnp.log(1.0 + jnp.exp(-jnp.abs(z)))
                    spm = sp if visible is None else jnp.where(visible, sp, 0.0)
                    hi, lo = _split(spm, 2)
                    cs = _dot(jnp.concatenate([hi, lo], axis=1), u_ref[...])
                    mid.append((z - sp, cs))
                mg.append(mid)
            mids.append(mg)
        states = []
        for (_, specs, state, _), mg in zip(groups, mids):
            for (_, vt_blk, _, width, visible, (r0, r1)), mid in zip(specs, mg):
                new_carries = []
                probs = []
                for h, (log_beta, cs) in enumerate(mid):
                    carry = state[h][r0:r1]
                    p = jnp.exp(log_beta + cs[:, :width] + carry[:, :width])
                    if visible is not None:
                        p = jnp.where(visible, p, 0.0)
                    probs.append(p.astype(BF16))
                    new_carries.append(carry + cs[:, width:])
                vz = jnp.zeros_like(vt_blk)
                vcat = jnp.concatenate([jnp.where(lo_rows, vt_blk, vz), jnp.where(lo_rows, vz, vt_blk)], axis=1)
                acc = state[2][r0:r1] + _dot_nt(jnp.concatenate(probs, axis=1), vcat)
                out = (new_carries[0], new_carries[1], acc)
                if (r0, r1) != (0, tq):
                    out = tuple(jnp.concatenate(([old[:r0]] if r0 else []) + [new] + ([old[r1:]] if r1 < tq else []),
                                                axis=0) for old, new in zip(state, out))
                state = out
            states.append(state)
        return states

    def live(state, rows):
        return jnp.max(jnp.maximum(state[0][rows[0]:rows[1]], state[1][rows[0]:rows[1]])) > EXIT_LOG

    def kv_block(k_ref, v_ref, unit, start, size):
        si, hp = unit
        if len(k_ref.shape) == 3:
            rows = slice(hp * LANES, (hp + 1) * LANES)
            return k_ref[si, rows, pl.ds(start, size)], v_ref[si, rows, pl.ds(start, size)]
        return tuple(r[si, 2 * hp:2 * hp + 2, :, pl.ds(start, size)].reshape(LANES, size).astype(BF16)
                     for r in (k_ref, v_ref))

    def sweep(q_heads, k_ref, v_ref, unit, n_blocks, skip, group, state):
        def cond(c):
            return jnp.logical_and(c[0] < (n_blocks - skip) // group, c[1])

        def body(c):
            i, _, st = c
            specs = []
            for g in range(group):
                start = pl.multiple_of((n_blocks - 1 - skip - (i * group + g)) * tk, tk)
                specs.append(kv_block(k_ref, v_ref, unit, start, tk) + (uo_ref, tk, None, (0, tq)))
            st = blocks([(q_heads, specs, st, [(0, 1)] if group == 2 else [])])[0]
            return i + 1, live(st, (0, tq)), st

        return lax.while_loop(cond, body, (jnp.int32(0), live(state, (0, tq)), state))[2]

    def tile_head(unit, qi, with_new):
        q0 = qi * tq if isinstance(qi, int) else pl.multiple_of(qi * tq, tq)
        q2 = q_ref[unit[0], pl.ds(q0, tq), unit[1] * LANES:(unit[1] + 1) * LANES]
        qz = jnp.zeros_like(q2)
        q_heads = (jnp.where(lo_lanes, q2, qz), jnp.where(lo_lanes, qz, q2))
        zeros = jnp.zeros((tq, LANES), F32)
        state = (zeros, zeros, zeros)
        specs = []
        for d in reversed(range(tq // tkd)):
            rows = tq - d * tkd
            visible = (lax.broadcasted_iota(jnp.int32, (rows, tkd), 1)
                       < lax.broadcasted_iota(jnp.int32, (rows, tkd), 0))
            specs.append(kv_block(kn_ref, vn_ref, unit, q0 + d * tkd, tkd) + (ud_ref, tkd, visible, (d * tkd, tq)))
        n_diag = len(specs)
        pairs = [(0, 1)] if n_diag == 2 else []
        late = []
        if with_new:
            for j in range(n_fast):
                kv = kv_block(kn_ref, vn_ref, unit, pl.multiple_of(q0 - (j + 1) * tk, tk), tk)
                rows_j = max(tq - tk * j - tk // 2, tk // 2)
                specs.append(kv + (uo_ref, tk, None, (0, rows_j)))
                if rows_j < tq:
                    late.append(kv + (uo_ref, tk, None, (rows_j, tq)))
            pairs.append((n_diag, n_diag + 1))
        elif n_past:
            for j in range(n_fast):
                specs.append(kv_block(kp_ref, vp_ref, unit, (n_past - 1 - j) * tk, tk) + (uo_ref, tk, None, (0, tq)))
            pairs.append((n_diag, n_diag + 1))
        return q0, q_heads, specs, pairs, late, state

    def tile_rest(unit, qi, q_heads, late, state, with_new):
        for spec in late:
            state = lax.cond(live(state, spec[5]), lambda st, spec=spec: blocks([(q_heads, [spec], st, [])])[0],
                             lambda st: st, state)
        if with_new:
            state = sweep(q_heads, kn_ref, vn_ref, unit, qi * (tq // tk), n_fast, tq // tk, state)
        if n_past:
            state = sweep(q_heads, kp_ref, vp_ref, unit, n_past, 0 if with_new else n_fast, 2, state)
        return state

    def tiles(qi, with_new):
        units = [(si, hp) for si in range(bs) for hp in range(hps)]
        heads = [tile_head(u, qi, with_new) for u in units]
        states = blocks([(q_heads, specs, state, pairs) for _, q_heads, specs, pairs, _, state in heads])
        if with_new or n_past:
            def rest(sts):
                return [tile_rest(u, qi, q_heads, late, st, with_new)
                        for u, (_, q_heads, _, _, late, _), st in zip(units, heads, sts)]

            top = functools.reduce(jnp.maximum, [c for st in states for c in st[:2]])
            states = lax.cond(jnp.max(top) > EXIT_LOG, rest, lambda sts: sts, states)
        for (si, hp), (q0, _, _, _, _, _), state in zip(units, heads, states):
            o_ref[si, pl.ds(q0, tq), hp * LANES:(hp + 1) * LANES] = state[2].astype(BF16)

    tiles(0, False)
    if nq > 1:
        def tile_body(qi, c):
            tiles(qi, True)
            return c

        lax.fori_loop(1, nq, tile_body, 0)


def _attention(q, kn, vn, kp, vp, tq, tk, hps, bs):
    b, l, _ = q.shape
    tkd = min(tk, tq)
    nq = l // tq
    n_past = 0 if kp is None else kp.shape[3] // tk
    assert (nq == 1 or tq == 2 * tk) and (n_past == 0 or (nq == 1 and n_past % 2 == 0 and n_past >= 2))
    w = hps * LANES
    seq = lambda n: pl.BlockSpec((bs, n, w), lambda bi, hi: (bi, 0, hi))
    seq_t = pl.BlockSpec((bs, w, l), lambda bi, hi: (bi, hi, 0))
    ud, uo = _cum_matrix(tkd), _cum_matrix(tk)
    in_specs = [seq(l), seq_t, seq_t]
    args = [q, kn, vn]
    if n_past:
        assert w == SB_WIDTH
        past = pl.BlockSpec((bs,) + kp.shape[1:], lambda bi, hi: (bi, 0, 0, 0))
        in_specs += [past, past]
        args += [kp, vp]
    in_specs += [_const_spec(ud.shape), _const_spec(uo.shape)]
    args += [ud, uo]
    return pl.pallas_call(
        functools.partial(_attn_kernel, tq=tq, tk=tk, tkd=tkd, n_past=n_past, nq=nq, hps=hps, bs=bs),
        grid=(b // bs, SB_WIDTH // w),
        in_specs=in_specs,
        out_specs=seq(l),
        out_shape=jax.ShapeDtypeStruct((b, l, SB_WIDTH), BF16),
        compiler_params=pltpu.CompilerParams(
            dimension_semantics=("parallel", "parallel"), vmem_limit_bytes=VMEM_LIMIT),
        name="sb_attn",
    )(*args)


def _ssd_kernel(xbc_ref, z_ref, dt_ref, cw_ref, cb_ref, dtb_ref, alog_ref, dskip_ref, nw_ref,
                conv0_ref, st0_ref, ltri_ref, expand_ref, eye_ref, bmask_ref,
                y_ref, convout_ref, stout_ref, xp_scr, st_scr, *, q, cps):
    j = pl.program_id(1)
    rows = q * cps

    @pl.when(j == 0)
    def _():
        st_scr[...] = st0_ref[0]
        xp_scr[CONV_PAD - (CONV_WIDTH - 1):CONV_PAD, :] = conv0_ref[0]

    xp_scr[CONV_PAD:CONV_PAD + rows, :] = xbc_ref[0]
    base = CONV_PAD - (CONV_WIDTH - 1)
    xp = xp_scr[...]
    conv = xp * cw_ref[0:1, :]
    for i in range(1, CONV_WIDTH):
        conv = pltpu.roll(conv, 1, 0) + xp * cw_ref[i:i + 1, :]
    conv = conv[CONV_PAD:] + cb_ref[...]
    tail = xp_scr[base + rows:CONV_PAD + rows, :]
    xp_scr[base:CONV_PAD, :] = tail
    convout_ref[0] = tail
    xact = _silu(conv)
    xs_all = xact[:, :SSM_WIDTH]
    b2_all = xact[:, SSM_WIDTH:SSM_WIDTH + LANES].astype(BF16)
    c2_all = xact[:, SSM_WIDTH + LANES:].astype(BF16)
    dt_all = _softplus(dt_ref[0] + dtb_ref[...])
    a_all = dt_all * (-jnp.exp(alog_ref[...]))

    ltri = ltri_ref[...]
    expand = expand_ref[...]
    lane = lax.broadcasted_iota(jnp.int32, (1, LANES), 1)
    lo_lanes = lane < SSM_STATE
    causal = lax.broadcasted_iota(jnp.int32, (q, q), 0) >= lax.broadcasted_iota(jnp.int32, (q, q), 1)
    xz = jnp.zeros((q, LANES), BF16)
    sls = [slice(c * q, (c + 1) * q) for c in range(cps)]
    acum = [_dot(ltri, jnp.concatenate(_split(a_all[sl], 3), axis=0)) for sl in sls]
    acum_p = [jnp.concatenate(_split(a, 3), axis=1) for a in acum]
    acum_t = [_dot_nt(eye_ref[...], p) for p in acum_p]
    acum_e = [_dot(p, expand) for p in acum_p]
    dt_e = [_dot(jnp.concatenate(_split(dt_all[sl], 3), axis=1), expand) for sl in sls]
    grams = []
    for sl in sls:
        c2b, b2 = c2_all[sl], b2_all[sl]
        cz = jnp.zeros_like(c2b)
        grams.append([_dot_nt(jnp.where(lo_lanes, c2b, cz) if g == 0 else jnp.where(lo_lanes, cz, c2b), b2)
                      for g in range(SSM_GROUPS)])
    xdt = [xs_all[sl] * d for sl, d in zip(sls, dt_e)]
    st = st_scr[...]
    y_off = []
    for c, sl in enumerate(sls):
        end_e = acum_e[c][q - 1:q, :]
        xw = (xdt[c] * jnp.exp(end_e - acum_e[c])).astype(BF16)
        y_off.append(_dot(c2_all[sl], st.astype(BF16)) * jnp.exp(acum_e[c]))
        st = st * jnp.exp(end_e) + bmask_ref[...] * _dot_tn(b2_all[sl], xw)
    ys = []
    for c in range(cps):
        xdt_b = xdt[c].astype(BF16)
        y_pairs = []
        for g in range(SSM_GROUPS):
            for pair in range(2):
                ms = []
                for hh in range(2):
                    h = 4 * g + 2 * pair + hh
                    d = acum[c][:, h:h + 1] - acum_t[c][h:h + 1, :]
                    decay = jnp.exp(jnp.where(causal, d, -1e30))
                    ms.append((grams[c][g] * decay).astype(BF16))
                xd = xdt_b[:, (2 * g + pair) * LANES:(2 * g + pair + 1) * LANES]
                xcat = jnp.concatenate([jnp.where(lo_lanes, xd, xz), jnp.where(lo_lanes, xz, xd)], axis=0)
                y_pairs.append(_dot(jnp.concatenate(ms, axis=1), xcat))
        ys.append(y_off[c] + jnp.concatenate(y_pairs, axis=1))
    st_scr[...] = st
    y = (ys[0] if cps == 1 else jnp.concatenate(ys, axis=0)) + dskip_ref[...] * xs_all
    y = y * _silu(z_ref[0])
    gw = SSM_WIDTH // SSM_GROUPS
    outs = []
    for g in range(SSM_GROUPS):
        yg = y[:, g * gw:(g + 1) * gw]
        outs.append(yg * lax.rsqrt(jnp.mean(yg * yg, axis=-1, keepdims=True) + EPS))
    y_ref[0] = (jnp.concatenate(outs, axis=1) * nw_ref[...]).astype(BF16)

    @pl.when(j == pl.num_programs(1) - 1)
    def _():
        stout_ref[0] = st


def _ssd(xbc, z, dt, conv_w, conv_b, dt_bias, a_log, d_skip, ssm_norm_w, conv0, st0, q, cps):
    b, l, _ = xbc.shape
    rows = q * cps
    tri = jnp.asarray(np.tile(np.tril(np.ones((q, q), np.float32)), (1, 3)), dtype=BF16)
    expand = np.zeros((LANES, SSM_WIDTH), np.float32)
    for h in range(SSM_HEADS):
        expand[h, h * SSM_HEAD_DIM:(h + 1) * SSM_HEAD_DIM] = 1.0
    expand = np.tile(expand, (3, 1))
    eye = jnp.asarray(np.tile(np.eye(LANES, dtype=np.float32), (1, 3)), dtype=BF16)
    pad = lambda v: jnp.pad(v.astype(F32), (0, LANES - v.shape[0])).reshape(1, LANES)
    seq = lambda n: pl.BlockSpec((1, rows, n), lambda bi, ji: (bi, ji, 0))
    per_b = lambda s: pl.BlockSpec((1,) + s, lambda bi, ji: (bi, 0, 0))
    consts = [
        conv_w.astype(F32), conv_b.astype(F32).reshape(1, CONV_CH), pad(dt_bias), pad(a_log),
        jnp.repeat(d_skip.astype(F32), SSM_HEAD_DIM).reshape(1, SSM_WIDTH),
        ssm_norm_w.astype(F32).reshape(1, SSM_WIDTH),
    ]
    mats = [tri, jnp.asarray(expand, dtype=BF16), eye, _state_block_mask()]
    return pl.pallas_call(
        functools.partial(_ssd_kernel, q=q, cps=cps),
        grid=(b, l // rows),
        in_specs=[seq(CONV_CH), seq(SSM_WIDTH), seq(DT_PAD)]
        + [_const_spec(c.shape) for c in consts]
        + [per_b((CONV_WIDTH - 1, CONV_CH)), per_b((LANES, SSM_WIDTH))]
        + [_const_spec(m.shape) for m in mats],
        out_specs=[seq(SSM_WIDTH), per_b((CONV_WIDTH - 1, CONV_CH)), per_b((LANES, SSM_WIDTH))],
        out_shape=[
            jax.ShapeDtypeStruct((b, l, SSM_WIDTH), BF16),
            jax.ShapeDtypeStruct((b, CONV_WIDTH - 1, CONV_CH), F32),
            jax.ShapeDtypeStruct((b, LANES, SSM_WIDTH), F32),
        ],
        scratch_shapes=[
            pltpu.VMEM((CONV_PAD + rows, CONV_CH), F32),
            pltpu.VMEM((LANES, SSM_WIDTH), F32),
        ],
        compiler_params=pltpu.CompilerParams(
            dimension_semantics=("parallel", "arbitrary"), vmem_limit_bytes=VMEM_LIMIT),
        name="conv_ssd",
    )(xbc, z, dt, *consts, conv0, st0, *mats)


def _state_block_mask():
    m = np.zeros((LANES, SSM_WIDTH), np.float32)
    per = SSM_WIDTH // SSM_GROUPS
    for g in range(SSM_GROUPS):
        m[g * SSM_STATE:(g + 1) * SSM_STATE, g * per:(g + 1) * per] = 1.0
    return jnp.asarray(m)


def _state_to_packed(h):
    b = h.shape[0]
    hg = SSM_HEADS // SSM_GROUPS
    x = h.astype(F32).reshape(b, SSM_GROUPS, hg, SSM_HEAD_DIM, SSM_STATE)
    x = x.transpose(0, 1, 4, 2, 3).reshape(b, SSM_GROUPS, SSM_STATE, hg * SSM_HEAD_DIM)
    z = jnp.zeros_like(x)
    rows = [jnp.concatenate([x[:, g] if g2 == g else z[:, g] for g2 in range(SSM_GROUPS)], axis=-1)
            for g in range(SSM_GROUPS)]
    return jnp.concatenate(rows, axis=1)


def _state_from_packed(st):
    b = st.shape[0]
    hg = SSM_HEADS // SSM_GROUPS
    per = hg * SSM_HEAD_DIM
    blocks = [st[:, g * SSM_STATE:(g + 1) * SSM_STATE, g * per:(g + 1) * per] for g in range(SSM_GROUPS)]
    x = jnp.stack(blocks, axis=1).reshape(b, SSM_GROUPS, SSM_STATE, hg, SSM_HEAD_DIM)
    return x.transpose(0, 1, 3, 4, 2).reshape(b, SSM_HEADS, SSM_HEAD_DIM, SSM_STATE)


def _tail_kernel(x_ref, o_ref, ys_ref, woa_ref, wob_ref, n2_ref, wup_ref, wdn_ref, fn_ref, y_ref, *, ff_chunk):
    tm = x_ref.shape[0]
    halves = [slice(0, tm // 2), slice(tm // 2, tm)] if tm >= 512 else [slice(0, tm)]
    hs = [x_ref[sl, :] + _dot(o_ref[sl, :], woa_ref[...]) + _dot(ys_ref[sl, :], wob_ref[...]) for sl in halves]
    hns = [_rms(h, n2_ref[...]).astype(BF16) for h in hs]
    ffns = [None] * len(halves)
    for c in range(0, D_FF, ff_chunk):
        us = [jnp.maximum(_dot(hn, wup_ref[:, c:c + ff_chunk]), 0.0) for hn in hns]
        for i, u in enumerate(us):
            d = _dot((u * u).astype(BF16), wdn_ref[c:c + ff_chunk, :])
            ffns[i] = d if ffns[i] is None else ffns[i] + d
    for sl, h, ffn in zip(halves, hs, ffns):
        y_ref[sl, :] = _rms(h + ffn, fn_ref[...])


def _tail(x2d, o2d, ys2d, wo_a, wo_b, n2, wup, wdn, fnw, tm, ff_chunk=1024):
    t = x2d.shape[0]
    row = lambda n: pl.BlockSpec((tm, n), lambda i: (i, 0))
    single = lambda s: pl.BlockSpec(s, lambda i: (0, 0), pipeline_mode=pl.Buffered(1))
    return pl.pallas_call(
        functools.partial(_tail_kernel, ff_chunk=ff_chunk),
        grid=(t // tm,),
        in_specs=[row(D_MODEL), row(SB_WIDTH), row(SSM_WIDTH),
                  single(wo_a.shape), single(wo_b.shape), single((1, D_MODEL)),
                  single(wup.shape), single(wdn.shape), single((1, D_MODEL))],
        out_specs=row(D_MODEL),
        out_shape=jax.ShapeDtypeStruct((t, D_MODEL), F32),
        compiler_params=pltpu.CompilerParams(
            dimension_semantics=("parallel",), vmem_limit_bytes=VMEM_LIMIT),
        name="out_ffn",
    )(x2d, o2d, ys2d, wo_a, wo_b, n2, wup, wdn, fnw)


def _pick(n, prefs):
    for p in prefs:
        if n % p == 0:
            return p
    raise ValueError(f"no tile in {prefs} divides {n}")


def _layer(x, k_past, v_past, conv0, st0, w):
    b, l, _ = x.shape
    t = b * l
    tm = _pick(t, (512, 256, 128, 64))
    if l % tm == 0:
        q, kt, vt, ktb, vtb, z, xbc, dt = _in_proj(x, w["norm1"], w["w_rest"], w["w_kv_t"], tm)
    else:
        outs = _in_proj(x.reshape(1, t, D_MODEL), w["norm1"], w["w_rest"], w["w_kv_t"], tm)
        split_t = lambda a: jnp.moveaxis(a.reshape(a.shape[1:-1] + (b, l)), -2, 0)
        q, z, xbc, dt = (a.reshape(b, l, a.shape[-1]) for a in (outs[0], outs[5], outs[6], outs[7]))
        kt, vt, ktb, vtb = (split_t(a) for a in outs[1:5])
    tq = _pick(l, (256, 128, 64))
    o = _attention(q, ktb, vtb, k_past, v_past, tq, LANES, SB_WIDTH // LANES, 2 if l < tq * 2 and b % 2 == 0 else 1)
    q_ssd = _pick(l, (128, 64))
    ys, conv_new, st_new = _ssd(xbc, z, dt, w["conv_w"], w["conv_b"], w["dt_bias"], w["a_log"],
                                w["d_skip"], w["ssm_norm_w"], conv0, st0, q_ssd, 8 if l % (8 * q_ssd) == 0 else 1)
    y = _tail(x.reshape(t, D_MODEL), o.reshape(t, SB_WIDTH), ys.reshape(t, SSM_WIDTH),
              w["wo_a"], w["wo_b"], w["norm2"], w["w_up"], w["w_down"], w["final"], _pick(t, (1024, 512, 256, 128, 64)))
    return y.reshape(b, l, D_MODEL), kt, vt, conv_new, _state_from_packed(st_new)


def kernel(x_prompt, x_sample, cache_k, cache_v, state_conv, state_ssm, norm1_w, w_in, conv_w, conv_b,
           dt_bias, a_log, d_skip, ssm_norm_w, w_out, norm2_w, w_up, w_down, final_norm_w):
    assert w_in.shape[0] == 1, "one layer: the final norm is fused into the layer's last kernel"
    bp = x_prompt.shape[0]
    in_cols = w_in.shape[-1]
    w = {
        "norm1": norm1_w[0].reshape(1, D_MODEL),
        "w_rest": jnp.pad(jnp.concatenate([w_in[0][:, :SB_WIDTH], w_in[0][:, 3 * SB_WIDTH:]], axis=1),
                          ((0, 0), (0, DT_PAD - SSM_HEADS))).astype(BF16),
        "w_kv_t": w_in[0][:, SB_WIDTH:3 * SB_WIDTH].T.astype(BF16),
        "conv_w": conv_w[0], "conv_b": conv_b[0], "dt_bias": dt_bias[0], "a_log": a_log[0],
        "d_skip": d_skip[0], "ssm_norm_w": ssm_norm_w[0],
        "wo_a": w_out[0, :SB_WIDTH].astype(BF16), "wo_b": w_out[0, SB_WIDTH:].astype(BF16),
        "norm2": norm2_w[0].reshape(1, D_MODEL),
        "w_up": w_up[0].astype(BF16), "w_down": w_down[0].astype(BF16),
        "final": final_norm_w.reshape(1, D_MODEL),
    }
    assert in_cols == 3 * SB_WIDTH + SSM_WIDTH + CONV_CH + SSM_HEADS
    conv_zero = jnp.zeros((bp, CONV_WIDTH - 1, CONV_CH), F32)
    st_zero = jnp.zeros((bp, LANES, SSM_WIDTH), F32)
    yp, kp, vp, cp, sp = _layer(x_prompt, None, None, conv_zero, st_zero, w)
    to_t = lambda a: jnp.transpose(a, (0, 2, 3, 1))
    from_t = lambda a: jnp.transpose(a, (0, 3, 1, 2))[None]
    ys, ks, vs, cs, ss = _layer(x_sample, to_t(cache_k[0]), to_t(cache_v[0]), state_conv[0],
                                _state_to_packed(state_ssm[0]), w)
    return (yp, ys, from_t(kp), from_t(vp), cp[None], sp[None], from_t(ks), from_t(vs), cs[None], ss[None])
```

```python
import functools

import numpy as np
import jax
import jax.numpy as jnp
from jax import lax
from jax.experimental import pallas as pl
from jax.experimental.pallas import tpu as pltpu

F32 = jnp.float32
BF16 = jnp.bfloat16

D_MODEL = 1024
SB_HEADS = 8
SB_HEAD_DIM = 64
SB_WIDTH = SB_HEADS * SB_HEAD_DIM
SSM_HEADS = 8
SSM_HEAD_DIM = 64
SSM_WIDTH = SSM_HEADS * SSM_HEAD_DIM
SSM_GROUPS = 2
SSM_STATE = 64
CONV_WIDTH = 4
CONV_CH = SSM_WIDTH + 2 * SSM_GROUPS * SSM_STATE
D_FF = 4 * D_MODEL
EPS = 1e-5

LANES = 128
CONV_PAD = 8
DT_PAD = LANES
VMEM_LIMIT = 56 * 1024 * 1024
NEG_LOG2E = -1.4426950408889634
EXIT_LOG = -105.0


def _dot(a, b):
    return jnp.dot(a, b, preferred_element_type=F32)


def _dot_nt(a, b):
    return lax.dot_general(a, b, (((1,), (1,)), ((), ())), preferred_element_type=F32)


def _dot_tn(a, b):
    return lax.dot_general(a, b, (((0,), (0,)), ((), ())), preferred_element_type=F32)


def _split(x, pieces):
    out = []
    r = x
    for i in range(pieces):
        p = r.astype(BF16)
        out.append(p)
        if i + 1 < pieces:
            r = r - p.astype(F32)
    return out


def _rms(x, w):
    return x * lax.rsqrt(jnp.mean(x * x, axis=-1, keepdims=True) + EPS) * w


def _softplus(x):
    return jnp.maximum(x, 0.0) + jnp.log1p(jnp.exp2(jnp.abs(x) * NEG_LOG2E))


def _silu(x):
    return x / (1.0 + jnp.exp2(x * NEG_LOG2E))


def _const_spec(shape):
    nd = len(shape)
    return pl.BlockSpec(shape, lambda *_: (0,) * nd)


def _in_proj_kernel(x_ref, nw_ref, w_ref, wkv_ref, q_ref, kt_ref, vt_ref, ktb_ref, vtb_ref, z_ref, xbc_ref, dt_ref):
    xn = _rms(x_ref[0], nw_ref[...]).astype(BF16)
    tm = xn.shape[0]
    c = 0
    q_ref[0] = (_dot(xn, w_ref[:, c:c + SB_WIDTH]) * (SB_HEAD_DIM ** -0.5)).astype(BF16)
    c += SB_WIDTH
    for r, (t_ref, tb_ref) in enumerate(((kt_ref, ktb_ref), (vt_ref, vtb_ref))):
        t = _dot_nt(wkv_ref[r * SB_WIDTH:(r + 1) * SB_WIDTH, :], xn)
        t_ref[0] = t.reshape(SB_HEADS, SB_HEAD_DIM, tm)
        tb_ref[0] = t.astype(BF16)
    z_ref[0] = _dot(xn, w_ref[:, c:c + SSM_WIDTH])
    c += SSM_WIDTH
    xbc_ref[0] = _dot(xn, w_ref[:, c:c + CONV_CH])
    c += CONV_CH
    dt_ref[0] = _dot(xn, w_ref[:, c:c + DT_PAD])


def _in_proj(x, norm_w, w_rest, w_kv_t, tm):
    b, l, _ = x.shape
    row = lambda n: pl.BlockSpec((1, tm, n), lambda i, j: (i, j, 0))
    outs = [
        ((l, SB_WIDTH), BF16, row(SB_WIDTH)),
        ((SB_HEADS, SB_HEAD_DIM, l), F32, pl.BlockSpec((1, SB_HEADS, SB_HEAD_DIM, tm), lambda i, j: (i, 0, 0, j))),
        ((SB_HEADS, SB_HEAD_DIM, l), F32, pl.BlockSpec((1, SB_HEADS, SB_HEAD_DIM, tm), lambda i, j: (i, 0, 0, j))),
        ((SB_WIDTH, l), BF16, pl.BlockSpec((1, SB_WIDTH, tm), lambda i, j: (i, 0, j))),
        ((SB_WIDTH, l), BF16, pl.BlockSpec((1, SB_WIDTH, tm), lambda i, j: (i, 0, j))),
        ((l, SSM_WIDTH), F32, row(SSM_WIDTH)), ((l, CONV_CH), F32, row(CONV_CH)), ((l, DT_PAD), F32, row(DT_PAD)),
    ]
    return pl.pallas_call(
        _in_proj_kernel,
        grid=(b, l // tm),
        in_specs=[row(D_MODEL), _const_spec((1, D_MODEL)), _const_spec(w_rest.shape), _const_spec(w_kv_t.shape)],
        out_specs=[spec for _, _, spec in outs],
        out_shape=[jax.ShapeDtypeStruct((b,) + shape, d) for shape, d, _ in outs],
        compiler_params=pltpu.CompilerParams(
            dimension_semantics=("parallel", "parallel"), vmem_limit_bytes=VMEM_LIMIT),
        name="in_proj",
    )(x, norm_w, w_rest, w_kv_t)


def _cum_matrix(tk):
    j = np.arange(tk)[:, None]
    s = np.arange(tk)[None, :]
    later = -(j > s).astype(np.float32)
    total = -np.ones((tk, LANES), np.float32)
    half = np.concatenate([later, total], axis=1)
    return jnp.asarray(np.concatenate([half, half], axis=0), dtype=BF16)


def _attn_kernel(*refs, tq, tk, tkd, n_past, nq, hps, bs):
    if n_past:
        q_ref, kn_ref, vn_ref, kp_ref, vp_ref, ud_ref, uo_ref, o_ref = refs
    else:
        q_ref, kn_ref, vn_ref, ud_ref, uo_ref, o_ref = refs
    lane = lax.broadcasted_iota(jnp.int32, (1, LANES), 1)
    lo_lanes = lane < SB_HEAD_DIM
    lo_rows = lax.broadcasted_iota(jnp.int32, (LANES, 1), 0) < SB_HEAD_DIM
    n_fast = 2

    def logits(q_heads, specs, pairs):
        out = [None] * len(specs)
        for i, j in pairs:
            (kt_i, _, _, _, _, (i0, i1)), (kt_j, _, _, width, _, (j0, j1)) = specs[i], specs[j]
            u0, u1 = min(i0, j0), max(i1, j1)
            both = [_dot(q_heads[h][u0:u1], jnp.concatenate([kt_j, kt_i], axis=1)) for h in range(2)]
            out[j] = [z[j0 - u0:j1 - u0, :width] for z in both]
            out[i] = [z[i0 - u0:i1 - u0, width:] for z in both]
        for n, (kt_blk, _, _, _, _, (r0, r1)) in enumerate(specs):
            if out[n] is None:
                out[n] = [_dot(q_heads[h][r0:r1], kt_blk) for h in range(2)]
        return out

    def blocks(groups):
        zs = [logits(q_heads, specs, pairs) for q_heads, specs, _, pairs in groups]
        mids = []
        for (_, specs, _, _), zg in zip(groups, zs):
            mg = []
            for (_, _, u_ref, width, visible, _), zb in zip(specs, zg):
                mid = []
                for z in zb:
                    sp = jnp.maximum(z, 0.0) + jnp.log(1.0 + jnp.exp2(jnp.abs(z) * NEG_LOG2E))
                    spm = sp if visible is None else jnp.where(visible, sp, 0.0)
                    hi, lo = _split(spm, 2)
                    cs = _dot(jnp.concatenate([hi, lo], axis=1), u_ref[...])
                    mid.append((z - sp, cs))
                mg.append(mid)
            mids.append(mg)
        states = []
        for (_, specs, state, _), mg in zip(groups, mids):
            for (_, vt_blk, _, width, visible, (r0, r1)), mid in zip(specs, mg):
                new_carries = []
                probs = []
                for h, (log_beta, cs) in enumerate(mid):
                    carry = state[h][r0:r1]
                    p = jnp.exp(log_beta + cs[:, :width] + carry[:, :width])
                    if visible is not None:
                        p = jnp.where(visible, p, 0.0)
                    probs.append(p.astype(BF16))
                    new_carries.append(carry + cs[:, width:])
                vz = jnp.zeros_like(vt_blk)
                vcat = jnp.concatenate([jnp.where(lo_rows, vt_blk, vz), jnp.where(lo_rows, vz, vt_blk)], axis=1)
                acc = state[2][r0:r1] + _dot_nt(jnp.concatenate(probs, axis=1), vcat)
                out = (new_carries[0], new_carries[1], acc)
                if (r0, r1) != (0, tq):
                    out = tuple(jnp.concatenate(([old[:r0]] if r0 else []) + [new] + ([old[r1:]] if r1 < tq else []),
                                                axis=0) for old, new in zip(state, out))
                state = out
            states.append(state)
        return states

    def live(state, rows):
        return jnp.max(jnp.maximum(state[0][rows[0]:rows[1]], state[1][rows[0]:rows[1]])) > EXIT_LOG

    def kv_block(k_ref, v_ref, unit, start, size):
        si, hp = unit
        if len(k_ref.shape) == 3:
            rows = slice(hp * LANES, (hp + 1) * LANES)
            return k_ref[si, rows, pl.ds(start, size)], v_ref[si, rows, pl.ds(start, size)]
        return tuple(r[si, 2 * hp:2 * hp + 2, :, pl.ds(start, size)].reshape(LANES, size).astype(BF16)
                     for r in (k_ref, v_ref))

    def sweep(q_heads, k_ref, v_ref, unit, n_blocks, skip, group, state):
        def cond(c):
            return jnp.logical_and(c[0] < (n_blocks - skip) // group, c[1])

        def body(c):
            i, _, st = c
            specs = []
            for g in range(group):
                start = pl.multiple_of((n_blocks - 1 - skip - (i * group + g)) * tk, tk)
                specs.append(kv_block(k_ref, v_ref, unit, start, tk) + (uo_ref, tk, None, (0, tq)))
            st = blocks([(q_heads, specs, st, [(0, 1)] if group == 2 else [])])[0]
            return i + 1, live(st, (0, tq)), st

        return lax.while_loop(cond, body, (jnp.int32(0), live(state, (0, tq)), state))[2]

    def tile_head(unit, qi, with_new):
        q0 = qi * tq if isinstance(qi, int) else pl.multiple_of(qi * tq, tq)
        q2 = q_ref[unit[0], pl.ds(q0, tq), unit[1] * LANES:(unit[1] + 1) * LANES]
        qz = jnp.zeros_like(q2)
        q_heads = (jnp.where(lo_lanes, q2, qz), jnp.where(lo_lanes, qz, q2))
        zeros = jnp.zeros((tq, LANES), F32)
        state = (zeros, zeros, zeros)
        specs = []
        for d in reversed(range(tq // tkd)):
            rows = tq - d * tkd
            visible = (lax.broadcasted_iota(jnp.int32, (rows, tkd), 1)
                       < lax.broadcasted_iota(jnp.int32, (rows, tkd), 0))
            specs.append(kv_block(kn_ref, vn_ref, unit, q0 + d * tkd, tkd) + (ud_ref, tkd, visible, (d * tkd, tq)))
        n_diag = len(specs)
        pairs = [(0, 1)] if n_diag == 2 else []
        late = []
        if with_new:
            for j in range(n_fast):
                kv = kv_block(kn_ref, vn_ref, unit, pl.multiple_of(q0 - (j + 1) * tk, tk), tk)
                rows_j = max(tq - tk * j - tk // 2, tk // 2)
                specs.append(kv + (uo_ref, tk, None, (0, rows_j)))
                if rows_j < tq:
                    late.append(kv + (uo_ref, tk, None, (rows_j, tq)))
            pairs.append((n_diag, n_diag + 1))
        elif n_past:
            for j in range(n_fast):
                specs.append(kv_block(kp_ref, vp_ref, unit, (n_past - 1 - j) * tk, tk) + (uo_ref, tk, None, (0, tq)))
            pairs.append((n_diag, n_diag + 1))
        return q0, q_heads, specs, pairs, late, state

    def tile_rest(unit, qi, q_heads, late, state, with_new):
        for spec in late:
            state = lax.cond(live(state, spec[5]), lambda st, spec=spec: blocks([(q_heads, [spec], st, [])])[0],
                             lambda st: st, state)
        if with_new:
            state = sweep(q_heads, kn_ref, vn_ref, unit, qi * (tq // tk), n_fast, tq // tk, state)
        if n_past:
            state = sweep(q_heads, kp_ref, vp_ref, unit, n_past, 0 if with_new else n_fast, 2, state)
        return state

    def tiles(qi, with_new):
        units = [(si, hp) for si in range(bs) for hp in range(hps)]
        heads = [tile_head(u, qi, with_new) for u in units]
        states = blocks([(q_heads, specs, state, pairs) for _, q_heads, specs, pairs, _, state in heads])
        if with_new or n_past:
            def rest(sts):
                return [tile_rest(u, qi, q_heads, late, st, with_new)
                        for u, (_, q_heads, _, _, late, _), st in zip(units, heads, sts)]

            top = functools.reduce(jnp.maximum, [c for st in states for c in st[:2]])
            states = lax.cond(jnp.max(top) > EXIT_LOG, rest, lambda sts: sts, states)
        for (si, hp), (q0, _, _, _, _, _), state in zip(units, heads, states):
            o_ref[si, pl.ds(q0, tq), hp * LANES:(hp + 1) * LANES] = state[2].astype(BF16)

    tiles(0, False)
    if nq > 1:
        def tile_body(qi, c):
            tiles(qi, True)
            return c

        lax.fori_loop(1, nq, tile_body, 0)


def _attention(q, kn, vn, kp, vp, tq, tk, hps, bs):
    b, l, _ = q.shape
    tkd = min(tk, tq)
    nq = l // tq
    n_past = 0 if kp is None else kp.shape[3] // tk
    assert (nq == 1 or tq == 2 * tk) and (n_past == 0 or (nq == 1 and n_past % 2 == 0 and n_past >= 2))
    w = hps * LANES
    seq = lambda n: pl.BlockSpec((bs, n, w), lambda bi, hi: (bi, 0, hi))
    seq_t = pl.BlockSpec((bs, w, l), lambda bi, hi: (bi, hi, 0))
    ud, uo = _cum_matrix(tkd), _cum_matrix(tk)
    in_specs = [seq(l), seq_t, seq_t]
    args = [q, kn, vn]
    if n_past:
        assert w == SB_WIDTH
        past = pl.BlockSpec((bs,) + kp.shape[1:], lambda bi, hi: (bi, 0, 0, 0))
        in_specs += [past, past]
        args += [kp, vp]
    in_specs += [_const_spec(ud.shape), _const_spec(uo.shape)]
    args += [ud, uo]
    return pl.pallas_call(
        functools.partial(_attn_kernel, tq=tq, tk=tk, tkd=tkd, n_past=n_past, nq=nq, hps=hps, bs=bs),
        grid=(b // bs, SB_WIDTH // w),
        in_specs=in_specs,
        out_specs=seq(l),
        out_shape=jax.ShapeDtypeStruct((b, l, SB_WIDTH), BF16),
        compiler_params=pltpu.CompilerParams(
            dimension_semantics=("parallel", "parallel"), vmem_limit_bytes=VMEM_LIMIT),
        name="sb_attn",
    )(*args)


def _ssd_kernel(xbc_ref, z_ref, dt_ref, cw_ref, cb_ref, dtb_ref, alog_ref, dskip_ref, nw_ref,
                conv0_ref, st0_ref, ltri_ref, expand_ref, eye_ref, bmask_ref,
                y_ref, convout_ref, stout_ref, xp_scr, st_scr, *, q, cps):
    j = pl.program_id(1)
    rows = q * cps

    @pl.when(j == 0)
    def _():
        st_scr[...] = st0_ref[0]
        xp_scr[CONV_PAD - (CONV_WIDTH - 1):CONV_PAD, :] = conv0_ref[0]

    xp_scr[CONV_PAD:CONV_PAD + rows, :] = xbc_ref[0]
    base = CONV_PAD - (CONV_WIDTH - 1)
    xp = xp_scr[...]
    conv = xp * cw_ref[0:1, :]
    for i in range(1, CONV_WIDTH):
        conv = pltpu.roll(conv, 1, 0) + xp * cw_ref[i:i + 1, :]
    conv = conv[CONV_PAD:] + cb_ref[...]
    tail = xp_scr[base + rows:CONV_PAD + rows, :]
    xp_scr[base:CONV_PAD, :] = tail
    convout_ref[0] = tail
    xact = _silu(conv)
    xs_all = xact[:, :SSM_WIDTH]
    b2_all = xact[:, SSM_WIDTH:SSM_WIDTH + LANES].astype(BF16)
    c2_all = xact[:, SSM_WIDTH + LANES:].astype(BF16)
    dt_all = _softplus(dt_ref[0] + dtb_ref[...])
    a_all = dt_all * (-jnp.exp(alog_ref[...]))

    ltri = ltri_ref[...]
    expand = expand_ref[...]
    lane = lax.broadcasted_iota(jnp.int32, (1, LANES), 1)
    lo_lanes = lane < SSM_STATE
    causal = lax.broadcasted_iota(jnp.int32, (q, q), 0) >= lax.broadcasted_iota(jnp.int32, (q, q), 1)
    xz = jnp.zeros((q, LANES), BF16)
    sls = [slice(c * q, (c + 1) * q) for c in range(cps)]
    acum = [_dot(ltri, jnp.concatenate(_split(a_all[sl], 3), axis=0)) for sl in sls]
    acum_p = [jnp.concatenate(_split(a, 3), axis=1) for a in acum]
    acum_t = [_dot_nt(eye_ref[...], p) for p in acum_p]
    acum_e = [_dot(p, expand) for p in acum_p]
    dt_e = [_dot(jnp.concatenate(_split(dt_all[sl], 3), axis=1), expand) for sl in sls]
    grams = []
    for sl in sls:
        c2b, b2 = c2_all[sl], b2_all[sl]
        cz = jnp.zeros_like(c2b)
        grams.append([_dot_nt(jnp.where(lo_lanes, c2b, cz) if g == 0 else jnp.where(lo_lanes, cz, c2b), b2)
                      for g in range(SSM_GROUPS)])
    xdt = [xs_all[sl] * d for sl, d in zip(sls, dt_e)]
    st = st_scr[...]
    y_off = []
    for c, sl in enumerate(sls):
        end_e = acum_e[c][q - 1:q, :]
        xw = (xdt[c] * jnp.exp(end_e - acum_e[c])).astype(BF16)
        y_off.append(_dot(c2_all[sl], st.astype(BF16)) * jnp.exp(acum_e[c]))
        st = st * jnp.exp(end_e) + bmask_ref[...] * _dot_tn(b2_all[sl], xw)
    ys = []
    for c in range(cps):
        xdt_b = xdt[c].astype(BF16)
        y_pairs = []
        for g in range(SSM_GROUPS):
            for pair in range(2):
                ms = []
                for hh in range(2):
                    h = 4 * g + 2 * pair + hh
                    d = acum[c][:, h:h + 1] - acum_t[c][h:h + 1, :]
                    decay = jnp.exp(jnp.where(causal, d, -1e30))
                    ms.append((grams[c][g] * decay).astype(BF16))
                xd = xdt_b[:, (2 * g + pair) * LANES:(2 * g + pair + 1) * LANES]
                xcat = jnp.concatenate([jnp.where(lo_lanes, xd, xz), jnp.where(lo_lanes, xz, xd)], axis=0)
                y_pairs.append(_dot(jnp.concatenate(ms, axis=1), xcat))
        ys.append(y_off[c] + jnp.concatenate(y_pairs, axis=1))
    st_scr[...] = st
    y = (ys[0] if cps == 1 else jnp.concatenate(ys, axis=0)) + dskip_ref[...] * xs_all
    y = y * _silu(z_ref[0])
    gw = SSM_WIDTH // SSM_GROUPS
    outs = []
    for g in range(SSM_GROUPS):
        yg = y[:, g * gw:(g + 1) * gw]
        outs.append(yg * lax.rsqrt(jnp.mean(yg * yg, axis=-1, keepdims=True) + EPS))
    y_ref[0] = (jnp.concatenate(outs, axis=1) * nw_ref[...]).astype(BF16)

    @pl.when(j == pl.num_programs(1) - 1)
    def _():
        stout_ref[0] = st


def _ssd(xbc, z, dt, conv_w, conv_b, dt_bias, a_log, d_skip, ssm_norm_w, conv0, st0, q, cps):
    b, l, _ = xbc.shape
    rows = q * cps
    tri = jnp.asarray(np.tile(np.tril(np.ones((q, q), np.float32)), (1, 3)), dtype=BF16)
    expand = np.zeros((LANES, SSM_WIDTH), np.float32)
    for h in range(SSM_HEADS):
        expand[h, h * SSM_HEAD_DIM:(h + 1) * SSM_HEAD_DIM] = 1.0
    expand = np.tile(expand, (3, 1))
    eye = jnp.asarray(np.tile(np.eye(LANES, dtype=np.float32), (1, 3)), dtype=BF16)
    pad = lambda v: jnp.pad(v.astype(F32), (0, LANES - v.shape[0])).reshape(1, LANES)
    seq = lambda n: pl.BlockSpec((1, rows, n), lambda bi, ji: (bi, ji, 0))
    per_b = lambda s: pl.BlockSpec((1,) + s, lambda bi, ji: (bi, 0, 0))
    consts = [
        conv_w.astype(F32), conv_b.astype(F32).reshape(1, CONV_CH), pad(dt_bias), pad(a_log),
        jnp.repeat(d_skip.astype(F32), SSM_HEAD_DIM).reshape(1, SSM_WIDTH),
        ssm_norm_w.astype(F32).reshape(1, SSM_WIDTH),
    ]
    mats = [tri, jnp.asarray(expand, dtype=BF16), eye, _state_block_mask()]
    return pl.pallas_call(
        functools.partial(_ssd_kernel, q=q, cps=cps),
        grid=(b, l // rows),
        in_specs=[seq(CONV_CH), seq(SSM_WIDTH), seq(DT_PAD)]
        + [_const_spec(c.shape) for c in consts]
        + [per_b((CONV_WIDTH - 1, CONV_CH)), per_b((LANES, SSM_WIDTH))]
        + [_const_spec(m.shape) for m in mats],
        out_specs=[seq(SSM_WIDTH), per_b((CONV_WIDTH - 1, CONV_CH)), per_b((LANES, SSM_WIDTH))],
        out_shape=[
            jax.ShapeDtypeStruct((b, l, SSM_WIDTH), BF16),
            jax.ShapeDtypeStruct((b, CONV_WIDTH - 1, CONV_CH), F32),
            jax.ShapeDtypeStruct((b, LANES, SSM_WIDTH), F32),
        ],
        scratch_shapes=[
            pltpu.VMEM((CONV_PAD + rows, CONV_CH), F32),
            pltpu.VMEM((LANES, SSM_WIDTH), F32),
        ],
        compiler_params=pltpu.CompilerParams(
            dimension_semantics=("parallel", "arbitrary"), vmem_limit_bytes=VMEM_LIMIT),
        name="conv_ssd",
    )(xbc, z, dt, *consts, conv0, st0, *mats)


def _state_block_mask():
    m = np.zeros((LANES, SSM_WIDTH), np.float32)
    per = SSM_WIDTH // SSM_GROUPS
    for g in range(SSM_GROUPS):
        m[g * SSM_STATE:(g + 1) * SSM_STATE, g * per:(g + 1) * per] = 1.0
    return jnp.asarray(m)


def _state_to_packed(h):
    b = h.shape[0]
    hg = SSM_HEADS // SSM_GROUPS
    x = h.astype(F32).reshape(b, SSM_GROUPS, hg, SSM_HEAD_DIM, SSM_STATE)
    x = x.transpose(0, 1, 4, 2, 3).reshape(b, SSM_GROUPS, SSM_STATE, hg * SSM_HEAD_DIM)
    z = jnp.zeros_like(x)
    rows = [jnp.concatenate([x[:, g] if g2 == g else z[:, g] for g2 in range(SSM_GROUPS)], axis=-1)
            for g in range(SSM_GROUPS)]
    return jnp.concatenate(rows, axis=1)


def _state_from_packed(st):
    b = st.shape[0]
    hg = SSM_HEADS // SSM_GROUPS
    per = hg * SSM_HEAD_DIM
    blocks = [st[:, g * SSM_STATE:(g + 1) * SSM_STATE, g * per:(g + 1) * per] for g in range(SSM_GROUPS)]
    x = jnp.stack(blocks, axis=1).reshape(b, SSM_GROUPS, SSM_STATE, hg, SSM_HEAD_DIM)
    return x.transpose(0, 1, 3, 4, 2).reshape(b, SSM_HEADS, SSM_HEAD_DIM, SSM_STATE)


def _tail_kernel(x_ref, o_ref, ys_ref, woa_ref, wob_ref, n2_ref, wup_ref, wdn_ref, fn_ref, y_ref, *, ff_chunk):
    tm = x_ref.shape[0]
    halves = [slice(0, tm // 2), slice(tm // 2, tm)] if tm >= 512 else [slice(0, tm)]
    hs = [x_ref[sl, :] + _dot(o_ref[sl, :], woa_ref[...]) + _dot(ys_ref[sl, :], wob_ref[...]) for sl in halves]
    hns = [_rms(h, n2_ref[...]).astype(BF16) for h in hs]
    ffns = [None] * len(halves)
    for c in range(0, D_FF, ff_chunk):
        us = [jnp.maximum(_dot(hn, wup_ref[:, c:c + ff_chunk]), 0.0) for hn in hns]
        for i, u in enumerate(us):
            d = _dot((u * u).astype(BF16), wdn_ref[c:c + ff_chunk, :])
            ffns[i] = d if ffns[i] is None else ffns[i] + d
    for sl, h, ffn in zip(halves, hs, ffns):
        y_ref[sl, :] = _rms(h + ffn, fn_ref[...])


def _tail(x2d, o2d, ys2d, wo_a, wo_b, n2, wup, wdn, fnw, tm, ff_chunk=1024):
    t = x2d.shape[0]
    row = lambda n: pl.BlockSpec((tm, n), lambda i: (i, 0))
    single = lambda s: pl.BlockSpec(s, lambda i: (0, 0), pipeline_mode=pl.Buffered(1))
    return pl.pallas_call(
        functools.partial(_tail_kernel, ff_chunk=ff_chunk),
        grid=(t // tm,),
        in_specs=[row(D_MODEL), row(SB_WIDTH), row(SSM_WIDTH),
                  single(wo_a.shape), single(wo_b.shape), single((1, D_MODEL)),
                  single(wup.shape), single(wdn.shape), single((1, D_MODEL))],
        out_specs=row(D_MODEL),
        out_shape=jax.ShapeDtypeStruct((t, D_MODEL), F32),
        compiler_params=pltpu.CompilerParams(
            dimension_semantics=("parallel",), vmem_limit_bytes=VMEM_LIMIT),
        name="out_ffn",
    )(x2d, o2d, ys2d, wo_a, wo_b, n2, wup, wdn, fnw)


def _pick(n, prefs):
    for p in prefs:
        if n % p == 0:
            return p
    raise ValueError(f"no tile in {prefs} divides {n}")


def _layer(x, k_past, v_past, conv0, st0, w):
    b, l, _ = x.shape
    t = b * l
    tm = _pick(t, (512, 256, 128, 64))
    if l % tm == 0:
        q, kt, vt, ktb, vtb, z, xbc, dt = _in_proj(x, w["norm1"], w["w_rest"], w["w_kv_t"], tm)
    else:
        outs = _in_proj(x.reshape(1, t, D_MODEL), w["norm1"], w["w_rest"], w["w_kv_t"], tm)
        split_t = lambda a: jnp.moveaxis(a.reshape(a.shape[1:-1] + (b, l)), -2, 0)
        q, z, xbc, dt = (a.reshape(b, l, a.shape[-1]) for a in (outs[0], outs[5], outs[6], outs[7]))
        kt, vt, ktb, vtb = (split_t(a) for a in outs[1:5])
    tq = _pick(l, (256, 128, 64))
    o = _attention(q, ktb, vtb, k_past, v_past, tq, LANES, SB_WIDTH // LANES, 2 if l < tq * 2 and b % 2 == 0 else 1)
    q_ssd = _pick(l, (128, 64))
    ys, conv_new, st_new = _ssd(xbc, z, dt, w["conv_w"], w["conv_b"], w["dt_bias"], w["a_log"],
                                w["d_skip"], w["ssm_norm_w"], conv0, st0, q_ssd, 8 if l % (8 * q_ssd) == 0 else 1)
    y = _tail(x.reshape(t, D_MODEL), o.reshape(t, SB_WIDTH), ys.reshape(t, SSM_WIDTH),
              w["wo_a"], w["wo_b"], w["norm2"], w["w_up"], w["w_down"], w["final"], _pick(t, (1024, 512, 256, 128, 64)))
    return y.reshape(b, l, D_MODEL), kt, vt, conv_new, _state_from_packed(st_new)


def kernel(x_prompt, x_sample, cache_k, cache_v, state_conv, state_ssm, norm1_w, w_in, conv_w, conv_b,
           dt_bias, a_log, d_skip, ssm_norm_w, w_out, norm2_w, w_up, w_down, final_norm_w):
    assert w_in.shape[0] == 1, "one layer: the final norm is fused into the layer's last kernel"
    bp = x_prompt.shape[0]
    in_cols = w_in.shape[-1]
    w = {
        "norm1": norm1_w[0].reshape(1, D_MODEL),
        "w_rest": jnp.pad(jnp.concatenate([w_in[0][:, :SB_WIDTH], w_in[0][:, 3 * SB_WIDTH:]], axis=1),
                          ((0, 0), (0, DT_PAD - SSM_HEADS))).astype(BF16),
        "w_kv_t": w_in[0][:, SB_WIDTH:3 * SB_WIDTH].T.astype(BF16),
        "conv_w": conv_w[0], "conv_b": conv_b[0], "dt_bias": dt_bias[0], "a_log": a_log[0],
        "d_skip": d_skip[0], "ssm_norm_w": ssm_norm_w[0],
        "wo_a": w_out[0, :SB_WIDTH].astype(BF16), "wo_b": w_out[0, SB_WIDTH:].astype(BF16),
        "norm2": norm2_w[0].reshape(1, D_MODEL),
        "w_up": w_up[0].astype(BF16), "w_down": w_down[0].astype(BF16),
        "final": final_norm_w.reshape(1, D_MODEL),
    }
    assert in_cols == 3 * SB_WIDTH + SSM_WIDTH + CONV_CH + SSM_HEADS
    conv_zero = jnp.zeros((bp, CONV_WIDTH - 1, CONV_CH), F32)
    st_zero = jnp.zeros((bp, LANES, SSM_WIDTH), F32)
    yp, kp, vp, cp, sp = _layer(x_prompt, None, None, conv_zero, st_zero, w)
    to_t = lambda a: jnp.transpose(a, (0, 2, 3, 1))
    from_t = lambda a: jnp.transpose(a, (0, 3, 1, 2))[None]
    ys, ks, vs, cs, ss = _layer(x_sample, to_t(cache_k[0]), to_t(cache_v[0]), state_conv[0],
                                _state_to_packed(state_ssm[0]), w)
    return (yp, ys, from_t(kp), from_t(vp), cp[None], sp[None], from_t(ks), from_t(vs), cs[None], ss[None])
```

```python
import functools

import numpy as np
import jax
import jax.numpy as jnp
from jax import lax
from jax.experimental import pallas as pl
from jax.experimental.pallas import tpu as pltpu

F32 = jnp.float32
BF16 = jnp.bfloat16

D_MODEL = 1024
SB_HEADS = 8
SB_HEAD_DIM = 64
SB_WIDTH = SB_HEADS * SB_HEAD_DIM
SSM_HEADS = 8
SSM_HEAD_DIM = 64
SSM_WIDTH = SSM_HEADS * SSM_HEAD_DIM
SSM_GROUPS = 2
SSM_STATE = 64
CONV_WIDTH = 4
CONV_CH = SSM_WIDTH + 2 * SSM_GROUPS * SSM_STATE
D_FF = 4 * D_MODEL
EPS = 1e-5

LANES = 128
CONV_PAD = 8
DT_PAD = LANES
VMEM_LIMIT = 56 * 1024 * 1024
NEG_LOG2E = -1.4426950408889634
EXIT_LOG = -105.0


def _dot(a, b):
    return jnp.dot(a, b, preferred_element_type=F32)


def _dot_nt(a, b):
    return lax.dot_general(a, b, (((1,), (1,)), ((), ())), preferred_element_type=F32)


def _dot_tn(a, b):
    return lax.dot_general(a, b, (((0,), (0,)), ((), ())), preferred_element_type=F32)


def _split(x, pieces):
    out = []
    r = x
    for i in range(pieces):
        p = r.astype(BF16)
        out.append(p)
        if i + 1 < pieces:
            r = r - p.astype(F32)
    return out


def _rms(x, w):
    return x * lax.rsqrt(jnp.mean(x * x, axis=-1, keepdims=True) + EPS) * w


def _softplus(x):
    return jnp.maximum(x, 0.0) + jnp.log1p(jnp.exp2(jnp.abs(x) * NEG_LOG2E))


def _silu(x):
    return x / (1.0 + jnp.exp2(x * NEG_LOG2E))


def _const_spec(shape):
    nd = len(shape)
    return pl.BlockSpec(shape, lambda *_: (0,) * nd)


def _in_proj_kernel(x_ref, nw_ref, w_ref, wkv_ref, q_ref, kt_ref, vt_ref, ktb_ref, vtb_ref, z_ref, xbc_ref, dt_ref):
    xn = _rms(x_ref[0], nw_ref[...]).astype(BF16)
    tm = xn.shape[0]
    c = 0
    q_ref[0] = (_dot(xn, w_ref[:, c:c + SB_WIDTH]) * (SB_HEAD_DIM ** -0.5)).astype(BF16)
    c += SB_WIDTH
    for r, (t_ref, tb_ref) in enumerate(((kt_ref, ktb_ref), (vt_ref, vtb_ref))):
        t = _dot_nt(wkv_ref[r * SB_WIDTH:(r + 1) * SB_WIDTH, :], xn)
        t_ref[0] = t.reshape(SB_HEADS, SB_HEAD_DIM, tm)
        tb_ref[0] = t.astype(BF16)
    z_ref[0] = _dot(xn, w_ref[:, c:c + SSM_WIDTH])
    c += SSM_WIDTH
    xbc_ref[0] = _dot(xn, w_ref[:, c:c + CONV_CH])
    c += CONV_CH
    dt_ref[0] = _dot(xn, w_ref[:, c:c + DT_PAD])


def _in_proj(x, norm_w, w_rest, w_kv_t, tm):
    b, l, _ = x.shape
    row = lambda n: pl.BlockSpec((1, tm, n), lambda i, j: (i, j, 0))
    outs = [
        ((l, SB_WIDTH), BF16, row(SB_WIDTH)),
        ((SB_HEADS, SB_HEAD_DIM, l), F32, pl.BlockSpec((1, SB_HEADS, SB_HEAD_DIM, tm), lambda i, j: (i, 0, 0, j))),
        ((SB_HEADS, SB_HEAD_DIM, l), F32, pl.BlockSpec((1, SB_HEADS, SB_HEAD_DIM, tm), lambda i, j: (i, 0, 0, j))),
        ((SB_WIDTH, l), BF16, pl.BlockSpec((1, SB_WIDTH, tm), lambda i, j: (i, 0, j))),
        ((SB_WIDTH, l), BF16, pl.BlockSpec((1, SB_WIDTH, tm), lambda i, j: (i, 0, j))),
        ((l, SSM_WIDTH), F32, row(SSM_WIDTH)), ((l, CONV_CH), F32, row(CONV_CH)), ((l, DT_PAD), F32, row(DT_PAD)),
    ]
    return pl.pallas_call(
        _in_proj_kernel,
        grid=(b, l // tm),
        in_specs=[row(D_MODEL), _const_spec((1, D_MODEL)), _const_spec(w_rest.shape), _const_spec(w_kv_t.shape)],
        out_specs=[spec for _, _, spec in outs],
        out_shape=[jax.ShapeDtypeStruct((b,) + shape, d) for shape, d, _ in outs],
        compiler_params=pltpu.CompilerParams(
            dimension_semantics=("parallel", "parallel"), vmem_limit_bytes=VMEM_LIMIT),
        name="in_proj",
    )(x, norm_w, w_rest, w_kv_t)


def _cum_matrix(tk):
    j = np.arange(tk)[:, None]
    s = np.arange(tk)[None, :]
    later = -(j > s).astype(np.float32)
    total = -np.ones((tk, LANES), np.float32)
    half = np.concatenate([later, total], axis=1)
    return jnp.asarray(np.concatenate([half, half], axis=0), dtype=BF16)


def _attn_kernel(*refs, tq, tk, tkd, n_past, nq, hps, bs):
    if n_past:
        q_ref, kn_ref, vn_ref, kp_ref, vp_ref, kh_ref, vh_ref, ud_ref, uo_ref, o_ref, kbuf, vbuf = refs
    else:
        q_ref, kn_ref, vn_ref, ud_ref, uo_ref, o_ref = refs
    lane = lax.broadcasted_iota(jnp.int32, (1, LANES), 1)
    lo_lanes = lane < SB_HEAD_DIM
    lo_rows = lax.broadcasted_iota(jnp.int32, (LANES, 1), 0) < SB_HEAD_DIM
    n_fast = 2

    def logits(q_heads, specs, pairs):
        out = [None] * len(specs)
        for i, j in pairs:
            (kt_i, _, _, _, _, (i0, i1)), (kt_j, _, _, width, _, (j0, j1)) = specs[i], specs[j]
            u0, u1 = min(i0, j0), max(i1, j1)
            both = [_dot(q_heads[h][u0:u1], jnp.concatenate([kt_j, kt_i], axis=1)) for h in range(2)]
            out[j] = [z[j0 - u0:j1 - u0, :width] for z in both]
            out[i] = [z[i0 - u0:i1 - u0, width:] for z in both]
        for n, (kt_blk, _, _, _, _, (r0, r1)) in enumerate(specs):
            if out[n] is None:
                out[n] = [_dot(q_heads[h][r0:r1], kt_blk) for h in range(2)]
        return out

    def blocks(groups):
        zs = [logits(q_heads, specs, pairs) for q_heads, specs, _, pairs in groups]
        mids = []
        for (_, specs, _, _), zg in zip(groups, zs):
            mg = []
            for (_, _, u_ref, width, visible, _), zb in zip(specs, zg):
                mid = []
                for z in zb:
                    sp = jnp.maximum(z, 0.0) + jnp.log(1.0 + jnp.exp2(jnp.abs(z) * NEG_LOG2E))
                    spm = sp if visible is None else jnp.where(visible, sp, 0.0)
                    hi, lo = _split(spm, 2)
                    cs = _dot(jnp.concatenate([hi, lo], axis=1), u_ref[...])
                    mid.append((z - sp, cs))
                mg.append(mid)
            mids.append(mg)
        states = []
        for (_, specs, state, _), mg in zip(groups, mids):
            for (_, vt_blk, _, width, visible, (r0, r1)), mid in zip(specs, mg):
                new_carries = []
                probs = []
                for h, (log_beta, cs) in enumerate(mid):
                    carry = state[h][r0:r1]
                    p = jnp.exp(log_beta + cs[:, :width] + carry[:, :width])
                    if visible is not None:
                        p = jnp.where(visible, p, 0.0)
                    probs.append(p.astype(BF16))
                    new_carries.append(carry + cs[:, width:])
                vz = jnp.zeros_like(vt_blk)
                vcat = jnp.concatenate([jnp.where(lo_rows, vt_blk, vz), jnp.where(lo_rows, vz, vt_blk)], axis=1)
                acc = state[2][r0:r1] + _dot_nt(jnp.concatenate(probs, axis=1), vcat)
                out = (new_carries[0], new_carries[1], acc)
                if (r0, r1) != (0, tq):
                    out = tuple(jnp.concatenate(([old[:r0]] if r0 else []) + [new] + ([old[r1:]] if r1 < tq else []),
                                                axis=0) for old, new in zip(state, out))
                state = out
            states.append(state)
        return states

    def live(state, rows):
        return jnp.max(jnp.maximum(state[0][rows[0]:rows[1]], state[1][rows[0]:rows[1]])) > EXIT_LOG

    def kv_block(k_ref, v_ref, unit, start, size):
        si, hp = unit
        if len(k_ref.shape) == 3:
            rows = slice(hp * LANES, (hp + 1) * LANES)
            return k_ref[si, rows, pl.ds(start, size)], v_ref[si, rows, pl.ds(start, size)]
        return tuple(r[si, 2 * hp:2 * hp + 2, :, pl.ds(start, size)].reshape(LANES, size).astype(BF16)
                     for r in (k_ref, v_ref))

    def sweep(q_heads, get_kv, n_trips, group, state):
        def cond(c):
            return jnp.logical_and(c[0] < n_trips, c[1])

        def body(c):
            i, _, st = c
            specs = [kv + (uo_ref, tk, None, (0, tq)) for kv in get_kv(i)]
            st = blocks([(q_heads, specs, st, [(0, 1)] if group == 2 else [])])[0]
            return i + 1, live(st, (0, tq)), st

        return lax.while_loop(cond, body, (jnp.int32(0), live(state, (0, tq)), state))[2]

    def new_kv(unit, n_blocks, group):
        def get(i):
            return [kv_block(kn_ref, vn_ref, unit, pl.multiple_of((n_blocks - 1 - n_fast - (i * group + g)) * tk, tk), tk)
                    for g in range(group)]
        return get

    def cache_kv(unit):
        si, hp = unit

        def get(i):
            start = pl.multiple_of((n_past - 2 - n_fast - 2 * i) * tk, tk)
            for src, dst in ((kh_ref, kbuf), (vh_ref, vbuf)):
                pltpu.sync_copy(src.at[si, pl.ds(2 * hp, 2), :, pl.ds(start, 2 * tk)], dst)
            return [tuple(buf[:, :, off:off + tk].reshape(LANES, tk).astype(BF16) for buf in (kbuf, vbuf))
                    for off in (tk, 0)]
        return get

    def tile_head(unit, qi, with_new):
        q0 = qi * tq if isinstance(qi, int) else pl.multiple_of(qi * tq, tq)
        q2 = q_ref[unit[0], pl.ds(q0, tq), unit[1] * LANES:(unit[1] + 1) * LANES]
        qz = jnp.zeros_like(q2)
        q_heads = (jnp.where(lo_lanes, q2, qz), jnp.where(lo_lanes, qz, q2))
        zeros = jnp.zeros((tq, LANES), F32)
        state = (zeros, zeros, zeros)
        specs = []
        for d in reversed(range(tq // tkd)):
            rows = tq - d * tkd
            visible = (lax.broadcasted_iota(jnp.int32, (rows, tkd), 1)
                       < lax.broadcasted_iota(jnp.int32, (rows, tkd), 0))
            specs.append(kv_block(kn_ref, vn_ref, unit, q0 + d * tkd, tkd) + (ud_ref, tkd, visible, (d * tkd, tq)))
        n_diag = len(specs)
        pairs = [(0, 1)] if n_diag == 2 else []
        late = []
        if with_new:
            for j in range(n_fast):
                kv = kv_block(kn_ref, vn_ref, unit, pl.multiple_of(q0 - (j + 1) * tk, tk), tk)
                rows_j = max(tq - tk * j - tk // 2, tk // 2)
                specs.append(kv + (uo_ref, tk, None, (0, rows_j)))
                if rows_j < tq:
                    late.append(kv + (uo_ref, tk, None, (rows_j, tq)))
            pairs.append((n_diag, n_diag + 1))
        elif n_past:
            for j in range(n_fast):
                specs.append(kv_block(kp_ref, vp_ref, unit, (n_fast - 1 - j) * tk, tk) + (uo_ref, tk, None, (0, tq)))
            pairs.append((n_diag, n_diag + 1))
        return q0, q_heads, specs, pairs, late, state

    def tile_rest(unit, qi, q_heads, late, state, with_new):
        for spec in late:
            state = lax.cond(live(state, spec[5]), lambda st, spec=spec: blocks([(q_heads, [spec], st, [])])[0],
                             lambda st: st, state)
        if with_new:
            group = tq // tk
            state = sweep(q_heads, new_kv(unit, qi * group, group), (qi * group - n_fast) // group, group, state)
        if n_past:
            state = sweep(q_heads, cache_kv(unit), (n_past - n_fast) // 2, 2, state)
        return state

    def tiles(qi, with_new):
        units = [(si, hp) for si in range(bs) for hp in range(hps)]
        heads = [tile_head(u, qi, with_new) for u in units]
        states = blocks([(q_heads, specs, state, pairs) for _, q_heads, specs, pairs, _, state in heads])
        if with_new or n_past:
            def rest(sts):
                return [tile_rest(u, qi, q_heads, late, st, with_new)
                        for u, (_, q_heads, _, _, late, _), st in zip(units, heads, sts)]

            top = functools.reduce(jnp.maximum, [c for st in states for c in st[:2]])
            states = lax.cond(jnp.max(top) > EXIT_LOG, rest, lambda sts: sts, states)
        for (si, hp), (q0, _, _, _, _, _), state in zip(units, heads, states):
            o_ref[si, pl.ds(q0, tq), hp * LANES:(hp + 1) * LANES] = state[2].astype(BF16)

    tiles(0, False)
    if nq > 1:
        def tile_body(qi, c):
            tiles(qi, True)
            return c

        lax.fori_loop(1, nq, tile_body, 0)


def _attention(q, kn, vn, kp, vp, tq, tk, hps, bs):
    b, l, _ = q.shape
    tkd = min(tk, tq)
    nq = l // tq
    n_past = 0 if kp is None else kp.shape[3] // tk
    assert (nq == 1 or tq == 2 * tk) and (n_past == 0 or (nq == 1 and n_past % 2 == 0 and n_past >= 2))
    w = hps * LANES
    seq = lambda n: pl.BlockSpec((bs, n, w), lambda bi, hi: (bi, 0, hi))
    seq_t = pl.BlockSpec((bs, w, l), lambda bi, hi: (bi, hi, 0))
    ud, uo = _cum_matrix(tkd), _cum_matrix(tk)
    in_specs = [seq(l), seq_t, seq_t]
    args = [q, kn, vn]
    scratch = []
    if n_past:
        assert w == SB_WIDTH
        n_fast = 2
        recent = pl.BlockSpec((bs,) + kp.shape[1:3] + (n_fast * tk,), lambda bi, hi: (bi, 0, 0, n_past // n_fast - 1))
        whole = pl.BlockSpec(memory_space=pl.ANY)
        in_specs += [recent, recent, whole, whole]
        args += [kp, vp, kp, vp]
        scratch = [pltpu.VMEM((2, SB_HEAD_DIM, 2 * tk), F32)] * 2
    in_specs += [_const_spec(ud.shape), _const_spec(uo.shape)]
    args += [ud, uo]
    return pl.pallas_call(
        functools.partial(_attn_kernel, tq=tq, tk=tk, tkd=tkd, n_past=n_past, nq=nq, hps=hps, bs=bs),
        grid=(b // bs, SB_WIDTH // w),
        in_specs=in_specs,
        scratch_shapes=scratch,
        out_specs=seq(l),
        out_shape=jax.ShapeDtypeStruct((b, l, SB_WIDTH), BF16),
        compiler_params=pltpu.CompilerParams(
            dimension_semantics=("parallel", "parallel"), vmem_limit_bytes=VMEM_LIMIT),
        name="sb_attn",
    )(*args)


def _ssd_kernel(xbc_ref, z_ref, dt_ref, cw_ref, cb_ref, dtb_ref, alog_ref, dskip_ref, nw_ref,
                conv0_ref, st0_ref, ltri_ref, expand_ref, eye_ref, bmask_ref,
                y_ref, convout_ref, stout_ref, xp_scr, st_scr, *, q, cps):
    j = pl.program_id(1)
    rows = q * cps

    @pl.when(j == 0)
    def _():
        st_scr[...] = st0_ref[0]
        xp_scr[CONV_PAD - (CONV_WIDTH - 1):CONV_PAD, :] = conv0_ref[0]

    xp_scr[CONV_PAD:CONV_PAD + rows, :] = xbc_ref[0]
    base = CONV_PAD - (CONV_WIDTH - 1)
    xp = xp_scr[...]
    conv = xp * cw_ref[0:1, :]
    for i in range(1, CONV_WIDTH):
        conv = pltpu.roll(conv, 1, 0) + xp * cw_ref[i:i + 1, :]
    conv = conv[CONV_PAD:] + cb_ref[...]
    tail = xp_scr[base + rows:CONV_PAD + rows, :]
    xp_scr[base:CONV_PAD, :] = tail
    convout_ref[0] = tail
    xact = _silu(conv)
    xs_all = xact[:, :SSM_WIDTH]
    b2_all = xact[:, SSM_WIDTH:SSM_WIDTH + LANES].astype(BF16)
    c2_all = xact[:, SSM_WIDTH + LANES:].astype(BF16)
    dt_all = _softplus(dt_ref[0] + dtb_ref[...])
    a_all = dt_all * (-jnp.exp(alog_ref[...]))

    ltri = ltri_ref[...]
    expand = expand_ref[...]
    lane = lax.broadcasted_iota(jnp.int32, (1, LANES), 1)
    lo_lanes = lane < SSM_STATE
    causal = lax.broadcasted_iota(jnp.int32, (q, q), 0) >= lax.broadcasted_iota(jnp.int32, (q, q), 1)
    xz = jnp.zeros((q, LANES), BF16)
    sls = [slice(c * q, (c + 1) * q) for c in range(cps)]
    acum = [_dot(ltri, jnp.concatenate(_split(a_all[sl], 3), axis=0)) for sl in sls]
    acum_p = [jnp.concatenate(_split(a, 3), axis=1) for a in acum]
    acum_t = [_dot_nt(eye_ref[...], p) for p in acum_p]
    acum_e = [_dot(p, expand) for p in acum_p]
    dt_e = [_dot(jnp.concatenate(_split(dt_all[sl], 3), axis=1), expand) for sl in sls]
    grams = []
    for sl in sls:
        c2b, b2 = c2_all[sl], b2_all[sl]
        cz = jnp.zeros_like(c2b)
        grams.append([_dot_nt(jnp.where(lo_lanes, c2b, cz) if g == 0 else jnp.where(lo_lanes, cz, c2b), b2)
                      for g in range(SSM_GROUPS)])
    xdt = [xs_all[sl] * d for sl, d in zip(sls, dt_e)]
    st = st_scr[...]
    y_off = []
    for c, sl in enumerate(sls):
        end_e = acum_e[c][q - 1:q, :]
        xw = (xdt[c] * jnp.exp(end_e - acum_e[c])).astype(BF16)
        y_off.append(_dot(c2_all[sl], st.astype(BF16)) * jnp.exp(acum_e[c]))
        st = st * jnp.exp(end_e) + bmask_ref[...] * _dot_tn(b2_all[sl], xw)
    ys = []
    for c in range(cps):
        xdt_b = xdt[c].astype(BF16)
        y_pairs = []
        for g in range(SSM_GROUPS):
            for pair in range(2):
                ms = []
                for hh in range(2):
                    h = 4 * g + 2 * pair + hh
                    d = acum[c][:, h:h + 1] - acum_t[c][h:h + 1, :]
                    decay = jnp.exp(jnp.where(causal, d, -1e30))
                    ms.append((grams[c][g] * decay).astype(BF16))
                xd = xdt_b[:, (2 * g + pair) * LANES:(2 * g + pair + 1) * LANES]
                xcat = jnp.concatenate([jnp.where(lo_lanes, xd, xz), jnp.where(lo_lanes, xz, xd)], axis=0)
                y_pairs.append(_dot(jnp.concatenate(ms, axis=1), xcat))
        ys.append(y_off[c] + jnp.concatenate(y_pairs, axis=1))
    st_scr[...] = st
    y = (ys[0] if cps == 1 else jnp.concatenate(ys, axis=0)) + dskip_ref[...] * xs_all
    y = y * _silu(z_ref[0])
    gw = SSM_WIDTH // SSM_GROUPS
    outs = []
    for g in range(SSM_GROUPS):
        yg = y[:, g * gw:(g + 1) * gw]
        outs.append(yg * lax.rsqrt(jnp.mean(yg * yg, axis=-1, keepdims=True) + EPS))
    y_ref[0] = (jnp.concatenate(outs, axis=1) * nw_ref[...]).astype(BF16)

    @pl.when(j == pl.num_programs(1) - 1)
    def _():
        stout_ref[0] = st


def _ssd(xbc, z, dt, conv_w, conv_b, dt_bias, a_log, d_skip, ssm_norm_w, conv0, st0, q, cps):
    b, l, _ = xbc.shape
    rows = q * cps
    tri = jnp.asarray(np.tile(np.tril(np.ones((q, q), np.float32)), (1, 3)), dtype=BF16)
    expand = np.zeros((LANES, SSM_WIDTH), np.float32)
    for h in range(SSM_HEADS):
        expand[h, h * SSM_HEAD_DIM:(h + 1) * SSM_HEAD_DIM] = 1.0
    expand = np.tile(expand, (3, 1))
    eye = jnp.asarray(np.tile(np.eye(LANES, dtype=np.float32), (1, 3)), dtype=BF16)
    pad = lambda v: jnp.pad(v.astype(F32), (0, LANES - v.shape[0])).reshape(1, LANES)
    seq = lambda n: pl.BlockSpec((1, rows, n), lambda bi, ji: (bi, ji, 0))
    per_b = lambda s: pl.BlockSpec((1,) + s, lambda bi, ji: (bi, 0, 0))
    consts = [
        conv_w.astype(F32), conv_b.astype(F32).reshape(1, CONV_CH), pad(dt_bias), pad(a_log),
        jnp.repeat(d_skip.astype(F32), SSM_HEAD_DIM).reshape(1, SSM_WIDTH),
        ssm_norm_w.astype(F32).reshape(1, SSM_WIDTH),
    ]
    mats = [tri, jnp.asarray(expand, dtype=BF16), eye, _state_block_mask()]
    return pl.pallas_call(
        functools.partial(_ssd_kernel, q=q, cps=cps),
        grid=(b, l // rows),
        in_specs=[seq(CONV_CH), seq(SSM_WIDTH), seq(DT_PAD)]
        + [_const_spec(c.shape) for c in consts]
        + [per_b((CONV_WIDTH - 1, CONV_CH)), per_b((LANES, SSM_WIDTH))]
        + [_const_spec(m.shape) for m in mats],
        out_specs=[seq(SSM_WIDTH), per_b((CONV_WIDTH - 1, CONV_CH)), per_b((LANES, SSM_WIDTH))],
        out_shape=[
            jax.ShapeDtypeStruct((b, l, SSM_WIDTH), BF16),
            jax.ShapeDtypeStruct((b, CONV_WIDTH - 1, CONV_CH), F32),
            jax.ShapeDtypeStruct((b, LANES, SSM_WIDTH), F32),
        ],
        scratch_shapes=[
            pltpu.VMEM((CONV_PAD + rows, CONV_CH), F32),
            pltpu.VMEM((LANES, SSM_WIDTH), F32),
        ],
        compiler_params=pltpu.CompilerParams(
            dimension_semantics=("parallel", "arbitrary"), vmem_limit_bytes=VMEM_LIMIT),
        name="conv_ssd",
    )(xbc, z, dt, *consts, conv0, st0, *mats)


def _state_block_mask():
    m = np.zeros((LANES, SSM_WIDTH), np.float32)
    per = SSM_WIDTH // SSM_GROUPS
    for g in range(SSM_GROUPS):
        m[g * SSM_STATE:(g + 1) * SSM_STATE, g * per:(g + 1) * per] = 1.0
    return jnp.asarray(m)


def _state_to_packed(h):
    b = h.shape[0]
    hg = SSM_HEADS // SSM_GROUPS
    x = h.astype(F32).reshape(b, SSM_GROUPS, hg, SSM_HEAD_DIM, SSM_STATE)
    x = x.transpose(0, 1, 4, 2, 3).reshape(b, SSM_GROUPS, SSM_STATE, hg * SSM_HEAD_DIM)
    z = jnp.zeros_like(x)
    rows = [jnp.concatenate([x[:, g] if g2 == g else z[:, g] for g2 in range(SSM_GROUPS)], axis=-1)
            for g in range(SSM_GROUPS)]
    return jnp.concatenate(rows, axis=1)


def _state_from_packed(st):
    b = st.shape[0]
    hg = SSM_HEADS // SSM_GROUPS
    per = hg * SSM_HEAD_DIM
    blocks = [st[:, g * SSM_STATE:(g + 1) * SSM_STATE, g * per:(g + 1) * per] for g in range(SSM_GROUPS)]
    x = jnp.stack(blocks, axis=1).reshape(b, SSM_GROUPS, SSM_STATE, hg, SSM_HEAD_DIM)
    return x.transpose(0, 1, 3, 4, 2).reshape(b, SSM_HEADS, SSM_HEAD_DIM, SSM_STATE)


def _tail_kernel(x_ref, o_ref, ys_ref, woa_ref, wob_ref, n2_ref, wup_ref, wdn_ref, fn_ref, y_ref, *, ff_chunk):
    tm = x_ref.shape[0]
    halves = [slice(0, tm // 2), slice(tm // 2, tm)] if tm >= 512 else [slice(0, tm)]
    hs = [x_ref[sl, :] + _dot(o_ref[sl, :], woa_ref[...]) + _dot(ys_ref[sl, :], wob_ref[...]) for sl in halves]
    hns = [_rms(h, n2_ref[...]).astype(BF16) for h in hs]
    ffns = [None] * len(halves)
    for c in range(0, D_FF, ff_chunk):
        us = [jnp.maximum(_dot(hn, wup_ref[:, c:c + ff_chunk]), 0.0) for hn in hns]
        for i, u in enumerate(us):
            d = _dot((u * u).astype(BF16), wdn_ref[c:c + ff_chunk, :])
            ffns[i] = d if ffns[i] is None else ffns[i] + d
    for sl, h, ffn in zip(halves, hs, ffns):
        y_ref[sl, :] = _rms(h + ffn, fn_ref[...])


def _tail(x2d, o2d, ys2d, wo_a, wo_b, n2, wup, wdn, fnw, tm, ff_chunk=1024):
    t = x2d.shape[0]
    row = lambda n: pl.BlockSpec((tm, n), lambda i: (i, 0))
    single = lambda s: pl.BlockSpec(s, lambda i: (0, 0), pipeline_mode=pl.Buffered(1))
    return pl.pallas_call(
        functools.partial(_tail_kernel, ff_chunk=ff_chunk),
        grid=(t // tm,),
        in_specs=[row(D_MODEL), row(SB_WIDTH), row(SSM_WIDTH),
                  single(wo_a.shape), single(wo_b.shape), single((1, D_MODEL)),
                  single(wup.shape), single(wdn.shape), single((1, D_MODEL))],
        out_specs=row(D_MODEL),
        out_shape=jax.ShapeDtypeStruct((t, D_MODEL), F32),
        compiler_params=pltpu.CompilerParams(
            dimension_semantics=("parallel",), vmem_limit_bytes=VMEM_LIMIT),
        name="out_ffn",
    )(x2d, o2d, ys2d, wo_a, wo_b, n2, wup, wdn, fnw)


def _pick(n, prefs):
    for p in prefs:
        if n % p == 0:
            return p
    raise ValueError(f"no tile in {prefs} divides {n}")


def _layer(x, k_past, v_past, conv0, st0, w):
    b, l, _ = x.shape
    t = b * l
    tm = _pick(t, (512, 256, 128, 64))
    if l % tm == 0:
        q, kt, vt, ktb, vtb, z, xbc, dt = _in_proj(x, w["norm1"], w["w_rest"], w["w_kv_t"], tm)
    else:
        outs = _in_proj(x.reshape(1, t, D_MODEL), w["norm1"], w["w_rest"], w["w_kv_t"], tm)
        split_t = lambda a: jnp.moveaxis(a.reshape(a.shape[1:-1] + (b, l)), -2, 0)
        q, z, xbc, dt = (a.reshape(b, l, a.shape[-1]) for a in (outs[0], outs[5], outs[6], outs[7]))
        kt, vt, ktb, vtb = (split_t(a) for a in outs[1:5])
    tq = _pick(l, (256, 128, 64))
    o = _attention(q, ktb, vtb, k_past, v_past, tq, LANES, SB_WIDTH // LANES, 2 if l < tq * 2 and b % 2 == 0 else 1)
    q_ssd = _pick(l, (128, 64))
    ys, conv_new, st_new = _ssd(xbc, z, dt, w["conv_w"], w["conv_b"], w["dt_bias"], w["a_log"],
                                w["d_skip"], w["ssm_norm_w"], conv0, st0, q_ssd, 8 if l % (8 * q_ssd) == 0 else 1)
    y = _tail(x.reshape(t, D_MODEL), o.reshape(t, SB_WIDTH), ys.reshape(t, SSM_WIDTH),
              w["wo_a"], w["wo_b"], w["norm2"], w["w_up"], w["w_down"], w["final"], _pick(t, (1024, 512, 256, 128, 64)))
    return y.reshape(b, l, D_MODEL), kt, vt, conv_new, _state_from_packed(st_new)


def kernel(x_prompt, x_sample, cache_k, cache_v, state_conv, state_ssm, norm1_w, w_in, conv_w, conv_b,
           dt_bias, a_log, d_skip, ssm_norm_w, w_out, norm2_w, w_up, w_down, final_norm_w):
    assert w_in.shape[0] == 1, "one layer: the final norm is fused into the layer's last kernel"
    bp = x_prompt.shape[0]
    in_cols = w_in.shape[-1]
    w = {
        "norm1": norm1_w[0].reshape(1, D_MODEL),
        "w_rest": jnp.pad(jnp.concatenate([w_in[0][:, :SB_WIDTH], w_in[0][:, 3 * SB_WIDTH:]], axis=1),
                          ((0, 0), (0, DT_PAD - SSM_HEADS))).astype(BF16),
        "w_kv_t": w_in[0][:, SB_WIDTH:3 * SB_WIDTH].T.astype(BF16),
        "conv_w": conv_w[0], "conv_b": conv_b[0], "dt_bias": dt_bias[0], "a_log": a_log[0],
        "d_skip": d_skip[0], "ssm_norm_w": ssm_norm_w[0],
        "wo_a": w_out[0, :SB_WIDTH].astype(BF16), "wo_b": w_out[0, SB_WIDTH:].astype(BF16),
        "norm2": norm2_w[0].reshape(1, D_MODEL),
        "w_up": w_up[0].astype(BF16), "w_down": w_down[0].astype(BF16),
        "final": final_norm_w.reshape(1, D_MODEL),
    }
    assert in_cols == 3 * SB_WIDTH + SSM_WIDTH + CONV_CH + SSM_HEADS
    conv_zero = jnp.zeros((bp, CONV_WIDTH - 1, CONV_CH), F32)
    st_zero = jnp.zeros((bp, LANES, SSM_WIDTH), F32)
    yp, kp, vp, cp, sp = _layer(x_prompt, None, None, conv_zero, st_zero, w)
    to_t = lambda a: jnp.transpose(a, (0, 2, 3, 1))
    from_t = lambda a: jnp.transpose(a, (0, 3, 1, 2))[None]
    ys, ks, vs, cs, ss = _layer(x_sample, to_t(cache_k[0]), to_t(cache_v[0]), state_conv[0],
                                _state_to_packed(state_ssm[0]), w)
    return (yp, ys, from_t(kp), from_t(vp), cp[None], sp[None], from_t(ks), from_t(vs), cs[None], ss[None])
```
